```python
import jax
import jax.numpy as jnp
from jax import lax
import numpy as np

D_MODEL = 1024
BATCH = 8
SEQ = 2048
DEPTH = 2
DEC_BATCH = 128
DEC_SEQ = 8
PAST_LEN = 16384
PAGE_SIZE = 128

HEAD_DIM = 64
MOBA_HEADS = 6
MOBA_KV_HEADS = 2
MOBA_GROUP = MOBA_HEADS // MOBA_KV_HEADS
MOBA_BLOCK = 256
MOBA_TOPK = 3
RET_HEADS = 4
RET_DK = 64
RET_DV = 64
RET_CHUNK = 128
MLA_HEADS = 6
MLA_Q_LORA = 256
MLA_KV_LORA = 128
MLA_NOPE = 64
MLA_ROPE = 32
MLA_V = 64
MLA_QK = MLA_NOPE + MLA_ROPE
MIX_WIDTH = MOBA_HEADS * HEAD_DIM + RET_HEADS * RET_DV + MLA_HEADS * MLA_V
IN_WIDTH = (MOBA_HEADS + 2 * MOBA_KV_HEADS) * HEAD_DIM + RET_HEADS * (2 * RET_DK + 2 * RET_DV) + MLA_Q_LORA + MLA_KV_LORA + MLA_ROPE
D_FF = 4 * D_MODEL
ROPE_THETA = 10000.0
EPS = 1e-6
Q_BLOCK = 128

kernel_name = 'hymba_moba_retnet_mla_step'


def rms_norm(x, g):
    xf = x.astype(jnp.float32)
    y = xf * lax.rsqrt(jnp.mean(xf * xf, axis=-1, keepdims=True) + EPS)
    return (y * g.astype(jnp.float32)).astype(x.dtype)


def rope(x, pos):
    half = x.shape[-1] // 2
    inv = ROPE_THETA ** (-jnp.arange(half, dtype=jnp.float32) / half)
    ang = pos.astype(jnp.float32)[:, None] * inv[None, :]
    cos = jnp.cos(ang)[:, None, :]
    sin = jnp.sin(ang)[:, None, :]
    xf = x.astype(jnp.float32)
    x1, x2 = xf[..., :half], xf[..., half:]
    return jnp.concatenate([x1 * cos - x2 * sin, x2 * cos + x1 * sin], axis=-1).astype(x.dtype)


def softmax_f32(s, dtype):
    return jax.nn.softmax(s.astype(jnp.float32), axis=-1).astype(dtype)


def ret_log_gamma():
    return jnp.log(1.0 - 2.0 ** (-5.0 - jnp.arange(RET_HEADS, dtype=jnp.float32)))


def project(h, pos, lp):
    B, S, _ = h.shape
    widths = [MOBA_HEADS * HEAD_DIM, MOBA_KV_HEADS * HEAD_DIM, MOBA_KV_HEADS * HEAD_DIM,
              RET_HEADS * RET_DK, RET_HEADS * RET_DK, RET_HEADS * RET_DV, RET_HEADS * RET_DV,
              MLA_Q_LORA, MLA_KV_LORA, MLA_ROPE]
    offs = []
    acc = 0
    for w in widths[:-1]:
        acc += w
        offs.append(acc)
    zqm, zkm, zvm, zqr, zkr, zvr, zgr, zql, zcl, zpl = jnp.split(h @ lp['w_in'], offs, axis=-1)
    heads = lambda a, n: a.reshape(B, S, n, -1)
    q_m = rope(rms_norm(heads(zqm, MOBA_HEADS), lp['g_moba_q']), pos)
    k_m = rope(rms_norm(heads(zkm, MOBA_KV_HEADS), lp['g_moba_k']), pos)
    v_m = heads(zvm, MOBA_KV_HEADS)
    q_r = rope(heads(zqr, RET_HEADS), pos)
    k_r = rope(heads(zkr, RET_HEADS), pos) * (RET_DK ** -0.5)
    v_r = heads(zvr, RET_HEADS)
    q_l = heads(rms_norm(zql, lp['g_mla_qlat']) @ lp['w_mla_q_up'], MLA_HEADS)
    q_l = jnp.concatenate([rms_norm(q_l[..., :MLA_NOPE], lp['g_mla_qn']),
                           rope(rms_norm(q_l[..., MLA_NOPE:], lp['g_mla_qp']), pos)], axis=-1)
    c_kv = rms_norm(zcl, lp['g_mla_kvlat'])
    k_pe = rope(rms_norm(zpl, lp['g_mla_kp'])[:, :, None, :], pos)[:, :, 0, :]
    return q_m, k_m, v_m, q_r, k_r, v_r, zgr, q_l, c_kv, k_pe


def attend_blocks(qg, k_own, v_own, own_mask, k_sel=None, v_sel=None, valid=None):
    scale = HEAD_DIM ** -0.5
    s_own = jnp.einsum('qhgd,khd->hgqk', qg, k_own) * scale
    s_own = jnp.where(own_mask, s_own, -jnp.inf)
    if k_sel is None:
        o = jnp.einsum('hgqk,khd->hgqd', softmax_f32(s_own, v_own.dtype), v_own)
    else:
        s_sel = jnp.einsum('qhgd,hgqnkd->hgqnk', qg, k_sel) * scale
        s_sel = jnp.where(valid[:, None], s_sel, -jnp.inf)
        n_sel, blk = s_sel.shape[-2:]
        s_all = jnp.concatenate([s_sel.reshape(s_sel.shape[:3] + (n_sel * blk,)), s_own], axis=-1)
        p = softmax_f32(s_all, v_own.dtype)
        p_sel = p[..., :n_sel * blk].reshape(s_sel.shape)
        o = (jnp.einsum('hgqnk,hgqnkd->hgqd', p_sel, v_sel)
             + jnp.einsum('hgqk,khd->hgqd', p[..., n_sel * blk:], v_own))
    return o.transpose(2, 0, 1, 3).reshape(qg.shape[0], -1)


def moba_prompt(q, k, v):
    B, S = q.shape[:2]
    n_blk = -(-S // MOBA_BLOCK)
    pad = n_blk * MOBA_BLOCK - S
    padw = ((0, 0), (0, pad), (0, 0), (0, 0))
    kb = jnp.pad(k, padw).reshape(B, n_blk, MOBA_BLOCK, MOBA_KV_HEADS, HEAD_DIM)
    vb = jnp.pad(v, padw).reshape(B, n_blk, MOBA_BLOCK, MOBA_KV_HEADS, HEAD_DIM)
    kmean = jnp.mean(kb.astype(jnp.float32), axis=2).astype(k.dtype)
    n_sel = min(MOBA_TOPK, (S - 1) // MOBA_BLOCK)
    nq = S // Q_BLOCK
    hid = jnp.arange(MOBA_KV_HEADS)[:, None, None, None]

    def one(i):
        b = i // nq
        q0 = (i % nq) * Q_BLOCK
        blk = q0 // MOBA_BLOCK
        qg = lax.dynamic_slice_in_dim(q[b], q0, Q_BLOCK, axis=0).reshape(Q_BLOCK, MOBA_KV_HEADS, MOBA_GROUP, HEAD_DIM)
        qpos = q0 + jnp.arange(Q_BLOCK)
        kpos = blk * MOBA_BLOCK + jnp.arange(MOBA_BLOCK)
        own_mask = kpos[None, :] <= qpos[:, None]
        kb_b, vb_b = kb[b], vb[b]
        if n_sel == 0:
            return attend_blocks(qg, kb_b[blk], vb_b[blk], own_mask)
        gate = jnp.einsum('qhgd,nhd->hgqn', qg, kmean[b]).astype(jnp.float32)
        gate = jnp.where(jnp.arange(n_blk) < blk, gate, -jnp.inf)
        _, gidx = lax.top_k(gate, n_sel)
        valid = jnp.arange(n_sel) < blk
        return attend_blocks(qg, kb_b[blk], vb_b[blk], own_mask,
                             kb_b[gidx, :, hid, :], vb_b[gidx, :, hid, :], valid)

    o = lax.map(one, jnp.arange(B * nq))
    return o.reshape(B, S, MOBA_HEADS * HEAD_DIM)


def moba_sample(q, k_new, v_new, k_pool, v_pool, page_table):
    T = q.shape[1]
    n_full = PAST_LEN // MOBA_BLOCK
    ppb = MOBA_BLOCK // PAGE_SIZE
    own_start = n_full * MOBA_BLOCK
    own_past = PAST_LEN - own_start
    n_sel = min(MOBA_TOPK, n_full)
    qpos = PAST_LEN + jnp.arange(T)
    kpos = own_start + jnp.arange(own_past + T)
    own_mask = kpos[None, :] <= qpos[:, None]
    hid = jnp.arange(MOBA_KV_HEADS)[:, None, None, None, None]

    def one(args):
        qs, kn, vn, pt = args
        qg = qs.reshape(T, MOBA_KV_HEADS, MOBA_GROUP, HEAD_DIM)
        k_own, v_own = kn, vn
        if own_past > 0:
            own_pages = pt[own_start // PAGE_SIZE:]
            k_own = jnp.concatenate([k_pool[own_pages].reshape(own_past, MOBA_KV_HEADS, HEAD_DIM), kn], axis=0)
            v_own = jnp.concatenate([v_pool[own_pages].reshape(own_past, MOBA_KV_HEADS, HEAD_DIM), vn], axis=0)
        if n_sel == 0:
            return attend_blocks(qg, k_own, v_own, own_mask)
        blk_pages = pt[:n_full * ppb].reshape(n_full, ppb)
        kpast = k_pool[blk_pages].reshape(n_full, MOBA_BLOCK, MOBA_KV_HEADS, HEAD_DIM)
        kmean = jnp.mean(kpast.astype(jnp.float32), axis=1).astype(kn.dtype)
        gate = jnp.einsum('thgd,nhd->hgtn', qg, kmean).astype(jnp.float32)
        _, gidx = lax.top_k(gate, n_sel)
        sel = blk_pages[gidx]
        shp = sel.shape[:4] + (MOBA_BLOCK, HEAD_DIM)
        k_sel = k_pool[sel, :, hid, :].reshape(shp)
        v_sel = v_pool[sel, :, hid, :].reshape(shp)
        return attend_blocks(qg, k_own, v_own, own_mask, k_sel, v_sel, jnp.ones((n_sel,), dtype=bool))

    return lax.map(one, (q, k_new, v_new, page_table))


def retention_chunk(q, k, v, state):
    lg = ret_log_gamma()
    C = q.shape[1]
    idx = jnp.arange(C, dtype=jnp.float32)
    diff = idx[:, None] - idx[None, :]
    decay = jnp.where(diff >= 0, jnp.exp(lg[:, None, None] * jnp.maximum(diff, 0.0)), 0.0)
    qf, kf, vf = q.astype(jnp.float32), k.astype(jnp.float32), v.astype(jnp.float32)
    s = jnp.einsum('bihd,bjhd->bhij', qf, kf) * decay
    o = (jnp.einsum('bhij,bjhe->bihe', s, vf)
         + jnp.einsum('bihd,bhde->bihe', qf, state) * jnp.exp(lg[None, :] * (idx[:, None] + 1.0))[None, :, :, None])
    w = jnp.exp(lg[:, None] * (C - 1.0 - idx)[None, :])
    state = state * jnp.exp(lg * C)[None, :, None, None] + jnp.einsum('bjhd,bjhe,hj->bhde', kf, vf, w)
    return o, state


def retention_prompt(q, k, v):
    B, S, H, dk = q.shape
    dv = v.shape[-1]
    C = min(RET_CHUNK, S)
    nc = S // C
    chunks = lambda a: a.reshape(B, nc, C, H, a.shape[-1]).swapaxes(0, 1)

    def step(state, inp):
        o, state = retention_chunk(inp[0], inp[1], inp[2], state)
        return state, o

    st, o = lax.scan(step, jnp.zeros((B, H, dk, dv), jnp.float32), (chunks(q), chunks(k), chunks(v)))
    return o.swapaxes(0, 1).reshape(B, S, H, dv).astype(v.dtype), st


def mla_expand(c, kpe, w_kv_up, g_kn):
    kv = (c @ w_kv_up).reshape(c.shape[:-1] + (MLA_HEADS, MLA_NOPE + MLA_V))
    k_nope = rms_norm(kv[..., :MLA_NOPE], g_kn)
    k_pe = jnp.broadcast_to(kpe[..., None, :], kv.shape[:-1] + (MLA_ROPE,)).astype(k_nope.dtype)
    return jnp.concatenate([k_nope, k_pe], axis=-1), kv[..., MLA_NOPE:]


def mla_prompt(q, c_kv, k_pe, w_kv_up, g_kn):
    B, S, H, _ = q.shape
    k, v = mla_expand(c_kv, k_pe, w_kv_up, g_kn)
    nq = S // Q_BLOCK
    pos = jnp.arange(S)
    qb = q.reshape(B, nq, Q_BLOCK, H, MLA_QK).swapaxes(0, 1)
    pb = pos.reshape(nq, Q_BLOCK)
    scale = MLA_QK ** -0.5

    def one(args):
        qc, pc = args
        s = jnp.einsum('bqhd,bkhd->bhqk', qc, k) * scale
        s = jnp.where(pos[None, :] <= pc[:, None], s, -jnp.inf)
        return jnp.einsum('bhqk,bkhe->bqhe', softmax_f32(s, v.dtype), v)

    o = lax.map(one, (qb, pb))
    return o.swapaxes(0, 1).reshape(B, S, H * MLA_V)


def mla_sample(q, c_new, kpe_new, lat_pool, kpe_pool, page_table, w_kv_up, g_kn):
    T = q.shape[1]
    qpos = PAST_LEN + jnp.arange(T)
    kpos = jnp.arange(PAST_LEN + T)
    mask = kpos[None, :] <= qpos[:, None]
    scale = MLA_QK ** -0.5

    def one(args):
        qs, cn, pn, pt = args
        c_all = jnp.concatenate([lat_pool[pt].reshape(PAST_LEN, MLA_KV_LORA), cn], axis=0)
        p_all = jnp.concatenate([kpe_pool[pt].reshape(PAST_LEN, MLA_ROPE), pn], axis=0)
        k, v = mla_expand(c_all, p_all, w_kv_up, g_kn)
        s = jnp.einsum('thd,khd->htk', qs, k) * scale
        s = jnp.where(mask, s, -jnp.inf)
        return jnp.einsum('htk,khe->the', softmax_f32(s, v.dtype), v).reshape(T, MLA_HEADS * MLA_V)

    return lax.map(one, (q, c_new, kpe_new, page_table))


def mix_out(o_m, o_r, g_r, o_l, lp):
    B, S = o_m.shape[:2]
    o_r = rms_norm(o_r, lp['g_ret_out']).reshape(B, S, -1) * jax.nn.silu(g_r)
    return jnp.concatenate([o_m, o_r, o_l], axis=-1) @ lp['w_out']


def mlp(h, lp):
    return jnp.square(jax.nn.relu(h @ lp['w_mlp_up'])) @ lp['w_mlp_down']


def layer_prompt(x, lp):
    S = x.shape[1]
    pos = jnp.arange(S)
    h = rms_norm(x, lp['g_attn'])
    q_m, k_m, v_m, q_r, k_r, v_r, g_r, q_l, c_kv, k_pe = project(h, pos, lp)
    o_m = moba_prompt(q_m, k_m, v_m)
    o_r, st = retention_prompt(q_r, k_r, v_r)
    o_l = mla_prompt(q_l, c_kv, k_pe, lp['w_mla_kv_up'], lp['g_mla_kn'])
    x = x + mix_out(o_m, o_r, g_r, o_l, lp)
    x = x + mlp(rms_norm(x, lp['g_mlp']), lp)
    return x, (k_m, v_m, c_kv, k_pe, st.astype(x.dtype))


def layer_sample(x, lp, ck, cv, clat, ckpe, st, page_table):
    T = x.shape[1]
    pos = PAST_LEN + jnp.arange(T)
    h = rms_norm(x, lp['g_attn'])
    q_m, k_m, v_m, q_r, k_r, v_r, g_r, q_l, c_kv, k_pe = project(h, pos, lp)
    o_m = moba_sample(q_m, k_m, v_m, ck, cv, page_table)
    o_r, st_new = retention_chunk(q_r, k_r, v_r, st.astype(jnp.float32))
    o_l = mla_sample(q_l, c_kv, k_pe, clat, ckpe, page_table, lp['w_mla_kv_up'], lp['g_mla_kn'])
    x = x + mix_out(o_m, o_r.astype(x.dtype), g_r, o_l, lp)
    x = x + mlp(rms_norm(x, lp['g_mlp']), lp)
    return x, (k_m, v_m, c_kv, k_pe, st_new.astype(st.dtype))


def setup_inputs(seed: int = 0) -> dict:
    key = jax.random.key(seed)
    ks = jax.random.split(key, 25)
    n_pages = PAST_LEN // PAGE_SIZE
    n_used = DEC_BATCH * n_pages
    n_pool = n_used + max(1, n_used // 4)
    f32 = jnp.float32
    nrm = lambda k, shape, scale=1.0: jax.random.normal(k, shape, f32) * scale
    gain = lambda k, n: 1.0 + 0.02 * jax.random.normal(k, (DEPTH, n), f32)
    page_table = jax.random.permutation(ks[7], n_pool)[:n_used].reshape(DEC_BATCH, n_pages).astype(jnp.int32)
    return {
        'x_prompt': nrm(ks[0], (BATCH, SEQ, D_MODEL)),
        'x_sample': nrm(ks[1], (DEC_BATCH, DEC_SEQ, D_MODEL)),
        'cache_moba_k': nrm(ks[2], (DEPTH, n_pool, PAGE_SIZE, MOBA_KV_HEADS, HEAD_DIM)),
        'cache_moba_v': nrm(ks[3], (DEPTH, n_pool, PAGE_SIZE, MOBA_KV_HEADS, HEAD_DIM)),
        'cache_mla_latent': nrm(ks[4], (DEPTH, n_pool, PAGE_SIZE, MLA_KV_LORA)),
        'cache_mla_kpe': nrm(ks[5], (DEPTH, n_pool, PAGE_SIZE, MLA_ROPE)),
        'state_ret': nrm(ks[6], (DEPTH, DEC_BATCH, RET_HEADS, RET_DK, RET_DV), 0.3),
        'page_table': page_table,
        'g_attn': gain(ks[8], D_MODEL),
        'w_in': nrm(ks[9], (DEPTH, D_MODEL, IN_WIDTH), D_MODEL ** -0.5),
        'g_moba_q': gain(ks[10], HEAD_DIM),
        'g_moba_k': gain(ks[11], HEAD_DIM),
        'g_ret_out': gain(ks[12], RET_DV),
        'g_mla_qlat': gain(ks[13], MLA_Q_LORA),
        'w_mla_q_up': nrm(ks[14], (DEPTH, MLA_Q_LORA, MLA_HEADS * MLA_QK), MLA_Q_LORA ** -0.5),
        'g_mla_qn': gain(ks[15], MLA_NOPE),
        'g_mla_qp': gain(ks[16], MLA_ROPE),
        'g_mla_kvlat': gain(ks[17], MLA_KV_LORA),
        'w_mla_kv_up': nrm(ks[18], (DEPTH, MLA_KV_LORA, MLA_HEADS * (MLA_NOPE + MLA_V)), MLA_KV_LORA ** -0.5),
        'g_mla_kn': gain(ks[19], MLA_NOPE),
        'g_mla_kp': gain(ks[20], MLA_ROPE),
        'w_out': nrm(ks[21], (DEPTH, MIX_WIDTH, D_MODEL), MIX_WIDTH ** -0.5),
        'g_mlp': gain(ks[22], D_MODEL),
        'w_mlp_up': nrm(ks[23], (DEPTH, D_MODEL, D_FF), D_MODEL ** -0.5),
        'w_mlp_down': nrm(ks[24], (DEPTH, D_FF, D_MODEL), D_FF ** -0.5),
    }


def reference(x_prompt, x_sample, cache_moba_k, cache_moba_v, cache_mla_latent, cache_mla_kpe, state_ret,
              page_table, g_attn, w_in, g_moba_q, g_moba_k, g_ret_out, g_mla_qlat, w_mla_q_up, g_mla_qn,
              g_mla_qp, g_mla_kvlat, w_mla_kv_up, g_mla_kn, g_mla_kp, w_out, g_mlp, w_mlp_up, w_mlp_down):
    yp, ys = x_prompt, x_sample
    new_p, new_s = [], []
    for l in range(DEPTH):
        lp = {'g_attn': g_attn[l], 'w_in': w_in[l], 'g_moba_q': g_moba_q[l], 'g_moba_k': g_moba_k[l],
              'g_ret_out': g_ret_out[l], 'g_mla_qlat': g_mla_qlat[l], 'w_mla_q_up': w_mla_q_up[l],
              'g_mla_qn': g_mla_qn[l], 'g_mla_qp': g_mla_qp[l], 'g_mla_kvlat': g_mla_kvlat[l],
              'w_mla_kv_up': w_mla_kv_up[l], 'g_mla_kn': g_mla_kn[l], 'g_mla_kp': g_mla_kp[l],
              'w_out': w_out[l], 'g_mlp': g_mlp[l], 'w_mlp_up': w_mlp_up[l], 'w_mlp_down': w_mlp_down[l]}
        yp, st_p = layer_prompt(yp, lp)
        ys, st_s = layer_sample(ys, lp, cache_moba_k[l], cache_moba_v[l], cache_mla_latent[l],
                                cache_mla_kpe[l], state_ret[l], page_table)
        new_p.append(st_p)
        new_s.append(st_s)
    mk_p, mv_p, ml_p, mp_p, rs_p = [jnp.stack(a) for a in zip(*new_p)]
    mk_s, mv_s, ml_s, mp_s, rs_s = [jnp.stack(a) for a in zip(*new_s)]
    return (yp, ys, mk_p, mv_p, ml_p, mp_p, rs_p, mk_s, mv_s, ml_s, mp_s, rs_s)
```

```python
import functools

import numpy as np
import jax
import jax.numpy as jnp
from jax import lax
from jax.experimental import pallas as pl
from jax.experimental.pallas import tpu as pltpu

HEAD_DIM = 64
MOBA_HEADS = 6
MOBA_KV_HEADS = 2
MOBA_GROUP = MOBA_HEADS // MOBA_KV_HEADS
MOBA_BLOCK = 256
MOBA_TOPK = 3
RET_HEADS = 4
RET_DK = 64
RET_DV = 64
RET_CHUNK = 128
MLA_HEADS = 6
MLA_Q_LORA = 256
MLA_KV_LORA = 128
MLA_NOPE = 64
MLA_ROPE = 32
MLA_V = 64
MLA_QK = MLA_NOPE + MLA_ROPE
ROPE_THETA = 10000.0
EPS = 1e-6

LANES = 128
VMEM_LIMIT = 56 * 1024 * 1024
F32 = jnp.float32
BF16 = jnp.bfloat16
NEG_INF = float("-inf")

N_IN_CHUNKS = 17


def _bf(x):
    return x.astype(BF16)


def _dot(a, b):
    return jnp.dot(a, b, preferred_element_type=F32)


def _dot_nt(a, b):
    return lax.dot_general(a, b, (((1,), (1,)), ((), ())), preferred_element_type=F32)


def _split_bf16(x):
    hi = _bf(x)
    lo = _bf(x - hi.astype(F32))
    return hi, lo


def _dot_nt_f32(a, b):
    ah, al = _split_bf16(a)
    bh, bl = _split_bf16(b)
    return _dot_nt(ah, bh) + (_dot_nt(ah, bl) + _dot_nt(al, bh))


def _dot_f32(a, b):
    ah, al = _split_bf16(a)
    bh, bl = _split_bf16(b)
    return _dot(ah, bh) + (_dot(ah, bl) + _dot(al, bh))


def _group_mean_sq(x, bd):
    hi, lo = _split_bf16(x * x)
    return _dot(hi, bd) + _dot(lo, bd)


def _lane(shape):
    return lax.broadcasted_iota(jnp.int32, shape, len(shape) - 1)


def _rotate_half(x, half):
    n = x.shape[-1]
    up = pltpu.roll(x, n - half, x.ndim - 1)
    down = pltpu.roll(x, half, x.ndim - 1)
    return jnp.where(_lane(x.shape) % (2 * half) < half, up, down)


def _rope(x, cos, sin_signed, half):
    return x * cos + _rotate_half(x, half) * sin_signed


def _row_rms(x):
    return x * lax.rsqrt(jnp.mean(x * x, axis=-1, keepdims=True) + EPS)


def _proj_kernel(x_ref, gattn_ref, win_ref, wq_ref, gains_ref, gqlat_ref, bd64_ref, bd32_ref,
                 cos64_ref, sin64_ref, cos32_ref, sin32_ref,
                 qm_ref, km_ref, vm_ref, qr_ref, kr_ref, vr_ref, gr_ref, ql_ref, ckv_ref, kpe_ref):
    x = x_ref[...]
    h = _row_rms(x) * gattn_ref[...]
    z = _dot(_bf(h), win_ref[...])
    bd64 = bd64_ref[...]
    bd32 = bd32_ref[...]
    cos64, sin64 = cos64_ref[...], sin64_ref[...]
    cos32, sin32 = cos32_ref[...], sin32_ref[...]
    g_mq, g_mk = gains_ref[0:1, :], gains_ref[1:2, :]
    g_kvlat, g_qn = gains_ref[2:3, :], gains_ref[3:4, :]
    g_qp, g_kp = gains_ref[4:5, :], gains_ref[5:6, :]

    def chunk(i):
        return z[:, i * LANES:(i + 1) * LANES]

    def norm_group(v, bd, g):
        return v * lax.rsqrt(_group_mean_sq(v, bd) + EPS) * g

    for i in range(3):
        qm_ref[:, i * LANES:(i + 1) * LANES] = _rope(norm_group(chunk(i), bd64, g_mq), cos64, sin64, 32)
    km_ref[...] = _rope(norm_group(chunk(3), bd64, g_mk), cos64, sin64, 32)
    vm_ref[...] = chunk(4)
    for i in range(2):
        qr_ref[:, i * LANES:(i + 1) * LANES] = _rope(chunk(5 + i), cos64, sin64, 32)
        kr_ref[:, i * LANES:(i + 1) * LANES] = _rope(chunk(7 + i), cos64, sin64, 32) * (RET_DK ** -0.5)
        vr_ref[:, i * LANES:(i + 1) * LANES] = chunk(9 + i)
        g = chunk(11 + i)
        gr_ref[:, i * LANES:(i + 1) * LANES] = g * jax.nn.sigmoid(g)
    zq = z[:, 13 * LANES:15 * LANES]
    ql = _dot(_bf(_row_rms(zq) * gqlat_ref[...]), wq_ref[...])
    for c in range(3):
        nope = ql[:, c * 256:c * 256 + LANES]
        pe = ql[:, c * 256 + LANES:(c + 1) * 256]
        ql_ref[:, c * 256:c * 256 + LANES] = norm_group(nope, bd64, g_qn)
        ql_ref[:, c * 256 + LANES:(c + 1) * 256] = _rope(norm_group(pe, bd32, g_qp), cos32, sin32, 16)
    ckv_ref[...] = _row_rms(chunk(15)) * g_kvlat
    kpe_ref[...] = _rope(norm_group(chunk(16), bd32, g_kp), cos32, sin32, 16)


def _in_proj_columns():
    cols = []
    for g in range(MOBA_GROUP):
        for hk in range(MOBA_KV_HEADS):
            h = hk * MOBA_GROUP + g
            cols += list(range(h * HEAD_DIM, (h + 1) * HEAD_DIM))
    cols += list(range(384, 2048))
    cols += list(range(2048, 2080)) * 2 + [-1] * 64
    return np.asarray(cols, np.int32)


def _q_up_columns():
    cols = []
    for c in range(MLA_HEADS // 2):
        for h in (2 * c, 2 * c + 1):
            cols += list(range(h * MLA_QK, h * MLA_QK + MLA_NOPE))
        for h in (2 * c, 2 * c + 1):
            cols += list(range(h * MLA_QK + MLA_NOPE, (h + 1) * MLA_QK))
        cols += [-1] * 64
    return np.asarray(cols, np.int32)


def _take_cols(w, cols):
    out = jnp.take(w, jnp.asarray(np.maximum(cols, 0)), axis=1)
    return out * jnp.asarray((cols >= 0).astype(np.float32))[None, :]


def _block_diag_avg(group):
    i = np.arange(LANES)
    return jnp.asarray(((i[:, None] // group) == (i[None, :] // group)).astype(np.float32) / group, BF16)


def _rope_tables(pos, dim):
    half = dim // 2
    inv = ROPE_THETA ** (-jnp.arange(half, dtype=F32) / half)
    ang = pos.astype(F32)[:, None] * inv[None, :]
    cos, sin = jnp.cos(ang), jnp.sin(ang)
    reps = LANES // dim
    return jnp.tile(jnp.concatenate([cos, cos], -1), (1, reps)), jnp.tile(jnp.concatenate([-sin, sin], -1), (1, reps))


def _tile_rows(n_rows, cap):
    t = min(cap, n_rows)
    assert n_rows % t == 0 and t % 8 == 0, (n_rows, t)
    return t


def _project(x2d, pos, n_pos_tiles, lw, tile):
    n, d = x2d.shape
    c64, s64 = _rope_tables(pos, 64)
    c32, s32 = _rope_tables(pos, 32)
    grid = (n // tile,)
    row = lambda w: pl.BlockSpec((tile, w), lambda i: (i, 0))
    const = lambda a: pl.BlockSpec(a.shape, lambda i: (0,) * a.ndim)
    tab = pl.BlockSpec((tile, LANES), lambda i: (i % n_pos_tiles, 0))
    widths = (384, 128, 128, 256, 256, 256, 256, 768, 128, 128)
    consts = (lw["g_attn"], lw["w_in"], lw["w_q_up"], lw["gains"], lw["g_qlat"], lw["bd64"], lw["bd32"])
    return pl.pallas_call(
        _proj_kernel,
        grid=grid,
        in_specs=[row(d)] + [const(a) for a in consts] + [tab] * 4,
        out_specs=[row(w) for w in widths],
        out_shape=[jax.ShapeDtypeStruct((n, w), F32) for w in widths],
        compiler_params=pltpu.CompilerParams(dimension_semantics=("arbitrary",), vmem_limit_bytes=VMEM_LIMIT),
        name="in_proj",
    )(x2d, *consts, c64, s64, c32, s32)


def _topk_mask(gate, n_valid, k):
    col = _lane(gate.shape)
    g = jnp.where(col < n_valid, gate, NEG_INF)
    sel = jnp.zeros(gate.shape, F32)
    big = jnp.int32(gate.shape[-1])
    for _ in range(k):
        m = jnp.max(g, axis=-1, keepdims=True)
        idx = jnp.min(jnp.where(g == m, col, big), axis=-1, keepdims=True)
        pick = (col == idx) & (m > NEG_INF)
        sel = jnp.where(pick, 1.0, sel)
        g = jnp.where(pick, NEG_INF, g)
    return sel


def _moba_prompt_kernel(q_ref, k_ref, v_ref, o_ref, kb_sc, vb_sc, kmean_sc, qs_sc, sel_sc, m_sc, l_sc, acc_sc):
    tq = q_ref.shape[0]
    n_blk = kmean_sc.shape[0]
    qi = pl.program_id(1)

    @pl.when(qi == 0)
    def _():
        kb_sc[...] = _bf(k_ref[...])
        vb_sc[...] = _bf(v_ref[...])
        for j in range(n_blk):
            kmean_sc[j:j + 1, :] = jnp.mean(k_ref[j * MOBA_BLOCK:(j + 1) * MOBA_BLOCK, :], axis=0, keepdims=True)

    lane = _lane((tq, LANES))
    kmean = kmean_sc[...]
    for g in range(MOBA_GROUP):
        chunk = q_ref[:, g * LANES:(g + 1) * LANES]
        for hk in range(MOBA_KV_HEADS):
            s = g * MOBA_KV_HEADS + hk
            q = jnp.where((lane // HEAD_DIM) == hk, chunk, 0.0)
            sel_sc[s * tq:(s + 1) * tq, :] = _topk_mask(_dot_nt_f32(q, kmean), qi, MOBA_TOPK)
            qs_sc[s * tq:(s + 1) * tq, :] = _bf(q * (HEAD_DIM ** -0.5))

    rows = qs_sc.shape[0]
    own = pl.multiple_of(qi * MOBA_BLOCK, MOBA_BLOCK)
    qs = qs_sc[...]
    s = _dot_nt(qs, kb_sc[pl.ds(own, MOBA_BLOCK), :])
    qpos = lax.broadcasted_iota(jnp.int32, s.shape, 0) % tq
    s = jnp.where(_lane(s.shape) <= qpos, s, NEG_INF)
    m = jnp.max(s, axis=-1, keepdims=True)
    p = jnp.exp(s - m)
    m_sc[...] = m
    l_sc[...] = jnp.sum(p, axis=-1, keepdims=True)
    acc_sc[...] = _dot(_bf(p), vb_sc[pl.ds(own, MOBA_BLOCK), :])

    def past(j, carry):
        start = pl.multiple_of(j * MOBA_BLOCK, MOBA_BLOCK)
        sel = sel_sc[...]
        chosen = jnp.sum(jnp.where(_lane(sel.shape) == j, sel, 0.0), axis=-1, keepdims=True) > 0.5
        s = jnp.where(chosen, _dot_nt(qs_sc[...], kb_sc[pl.ds(start, MOBA_BLOCK), :]), NEG_INF)
        m_old = m_sc[...]
        m_new = jnp.maximum(m_old, jnp.max(s, axis=-1, keepdims=True))
        alpha = jnp.exp(m_old - m_new)
        p = jnp.exp(s - m_new)
        m_sc[...] = m_new
        l_sc[...] = alpha * l_sc[...] + jnp.sum(p, axis=-1, keepdims=True)
        acc_sc[...] = alpha * acc_sc[...] + _dot(_bf(p), vb_sc[pl.ds(start, MOBA_BLOCK), :])
        return carry

    lax.fori_loop(0, qi, past, 0)

    for g in range(MOBA_GROUP):
        r0, r1 = (2 * g) * tq, (2 * g + 1) * tq
        o0 = acc_sc[r0:r0 + tq, :] / l_sc[r0:r0 + tq, :]
        o1 = acc_sc[r1:r1 + tq, :] / l_sc[r1:r1 + tq, :]
        o_ref[:, g * LANES:(g + 1) * LANES] = jnp.where(lane < HEAD_DIM, o0, o1)
    del rows


def _moba_prompt(q_m, k_m, v_m, batch, seq):
    assert seq % MOBA_BLOCK == 0
    tq = MOBA_BLOCK
    n_blk = seq // MOBA_BLOCK
    nq = seq // tq
    rows = MOBA_HEADS * tq
    return pl.pallas_call(
        _moba_prompt_kernel,
        grid=(batch, nq),
        in_specs=[pl.BlockSpec((tq, 384), lambda b, i: (b * nq + i, 0)),
                  pl.BlockSpec((seq, LANES), lambda b, i: (b, 0)),
                  pl.BlockSpec((seq, LANES), lambda b, i: (b, 0))],
        out_specs=pl.BlockSpec((tq, 384), lambda b, i: (b * nq + i, 0)),
        out_shape=jax.ShapeDtypeStruct((batch * seq, 384), F32),
        scratch_shapes=[pltpu.VMEM((seq, LANES), BF16), pltpu.VMEM((seq, LANES), BF16),
                        pltpu.VMEM((n_blk, LANES), F32), pltpu.VMEM((rows, LANES), BF16),
                        pltpu.VMEM((rows, n_blk), F32), pltpu.VMEM((rows, 1), F32),
                        pltpu.VMEM((rows, 1), F32), pltpu.VMEM((rows, LANES), F32)],
        compiler_params=pltpu.CompilerParams(dimension_semantics=("arbitrary", "arbitrary"),
                                             vmem_limit_bytes=VMEM_LIMIT),
        name="moba_prompt",
    )(q_m, k_m, v_m)


def _mla_slot_mask(hh):
    lane = _lane((1, 256))
    nope = (lane < LANES) & ((lane // MLA_NOPE) == hh)
    pe = (lane >= LANES) & (lane < LANES + 2 * MLA_ROPE) & (((lane - LANES) // MLA_ROPE) == hh)
    return nope | pe


def _mla_prompt_kernel(q_ref, ckv_ref, kpe_ref, wkv_ref, gkn_ref, bd64_ref, o_ref,
                       kcat_sc, v_sc, qs_sc, m_sc, l_sc, acc_sc):
    tq = q_ref.shape[0]
    seq = ckv_ref.shape[0]
    n_pairs = MLA_HEADS // 2
    qi = pl.program_id(1)

    @pl.when(qi == 0)
    def _():
        def expand(t, carry):
            r = pl.ds(pl.multiple_of(t * tq, tq), tq)
            kv = _dot(_bf(ckv_ref[r, :]), wkv_ref[...])
            pe = _bf(kpe_ref[r, :])
            for c in range(n_pairs):
                kn = kv[:, c * LANES:(c + 1) * LANES]
                kn = kn * lax.rsqrt(_group_mean_sq(kn, bd64_ref[...]) + EPS) * gkn_ref[...]
                kcat_sc[c, r, 0:LANES] = _bf(kn)
                kcat_sc[c, r, LANES:2 * LANES] = pe
                v_sc[c, r, :] = _bf(kv[:, (n_pairs + c) * LANES:(n_pairs + c + 1) * LANES])
            return carry
        lax.fori_loop(0, seq // tq, expand, 0)

    for c in range(n_pairs):
        chunk = q_ref[:, c * 256:(c + 1) * 256] * (MLA_QK ** -0.5)
        for hh in range(2):
            qs_sc[c, hh * tq:(hh + 1) * tq, :] = _bf(jnp.where(_mla_slot_mask(hh), chunk, 0.0))

    own = pl.multiple_of(qi * tq, tq)
    for c in range(n_pairs):
        s = _dot_nt(qs_sc[c], kcat_sc[c, pl.ds(own, tq), :])
        qpos = lax.broadcasted_iota(jnp.int32, s.shape, 0) % tq
        s = jnp.where(_lane(s.shape) <= qpos, s, NEG_INF)
        m = jnp.max(s, axis=-1, keepdims=True)
        p = jnp.exp(s - m)
        m_sc[c] = m
        l_sc[c] = jnp.sum(p, axis=-1, keepdims=True)
        acc_sc[c] = _dot(_bf(p), v_sc[c, pl.ds(own, tq), :])

    def past(j, carry):
        start = pl.multiple_of(j * tq, tq)
        for c in range(n_pairs):
            s = _dot_nt(qs_sc[c], kcat_sc[c, pl.ds(start, tq), :])
            m_old = m_sc[c]
            m_new = jnp.maximum(m_old, jnp.max(s, axis=-1, keepdims=True))
            alpha = jnp.exp(m_old - m_new)
            p = jnp.exp(s - m_new)
            m_sc[c] = m_new
            l_sc[c] = alpha * l_sc[c] + jnp.sum(p, axis=-1, keepdims=True)
            acc_sc[c] = alpha * acc_sc[c] + _dot(_bf(p), v_sc[c, pl.ds(start, tq), :])
        return carry

    lax.fori_loop(0, qi, past, 0)

    lane = _lane((tq, LANES))
    for c in range(n_pairs):
        o0 = acc_sc[c, 0:tq, :] / l_sc[c, 0:tq, :]
        o1 = acc_sc[c, tq:2 * tq, :] / l_sc[c, tq:2 * tq, :]
        o_ref[:, c * LANES:(c + 1) * LANES] = jnp.where(lane < MLA_V, o0, o1)


def _mla_prompt(q_l, c_kv, kpe_slot, lw, batch, seq):
    tq = 256
    assert seq % tq == 0
    nq = seq // tq
    n_pairs = MLA_HEADS // 2
    const = lambda a: pl.BlockSpec(a.shape, lambda b, i: (0,) * a.ndim)
    return pl.pallas_call(
        _mla_prompt_kernel,
        grid=(batch, nq),
        in_specs=[pl.BlockSpec((tq, 768), lambda b, i: (b * nq + i, 0)),
                  pl.BlockSpec((seq, LANES), lambda b, i: (b, 0)),
                  pl.BlockSpec((seq, LANES), lambda b, i: (b, 0)),
                  const(lw["w_kv_up"]), const(lw["g_kn"]), const(lw["bd64"])],
        out_specs=pl.BlockSpec((tq, 384), lambda b, i: (b * nq + i, 0)),
        out_shape=jax.ShapeDtypeStruct((batch * seq, 384), F32),
        scratch_shapes=[pltpu.VMEM((n_pairs, seq, 256), BF16), pltpu.VMEM((n_pairs, seq, LANES), BF16),
                        pltpu.VMEM((n_pairs, 2 * tq, 256), BF16), pltpu.VMEM((n_pairs, 2 * tq, 1), F32),
                        pltpu.VMEM((n_pairs, 2 * tq, 1), F32), pltpu.VMEM((n_pairs, 2 * tq, LANES), F32)],
        compiler_params=pltpu.CompilerParams(dimension_semantics=("arbitrary", "arbitrary"),
                                             vmem_limit_bytes=VMEM_LIMIT),
        name="mla_prompt",
    )(q_l, c_kv, kpe_slot, lw["w_kv_up"], lw["g_kn"], lw["bd64"])


def _retention_kernel(q_ref, k_ref, v_ref, gate_ref, st_ref, dec_ref, gpow_ref, wts_ref, gc_ref, bdm_ref,
                      gout_ref, bd64_ref, o_ref, sto_ref, st_sc):
    chunk = dec_ref.shape[-1]
    n_chunks = q_ref.shape[0] // chunk
    lane = _lane((chunk, LANES))
    st_sc[...] = st_ref[0, 0]

    def step(t, carry):
        r = pl.ds(pl.multiple_of(t * chunk, chunk), chunk)
        q, k, v = q_ref[r, :], k_ref[r, :], v_ref[r, :]
        kb, vb = _bf(k), _bf(v)
        halves = []
        for hh in range(2):
            qh = _bf(jnp.where((lane // RET_DK) == hh, q, 0.0))
            s = _dot_nt(qh, kb) * dec_ref[0, hh]
            halves.append(_dot(_bf(s), vb))
        o = jnp.where(lane < RET_DV, halves[0], halves[1])
        o = o + _dot(_bf(q), _bf(st_sc[...])) * gpow_ref[0]
        u = _dot(_bf((k * wts_ref[0]).T), vb)
        st_sc[...] = st_sc[...] * gc_ref[0] + u * bdm_ref[...]
        o = o * lax.rsqrt(_group_mean_sq(o, bd64_ref[...]) + EPS) * gout_ref[...]
        o_ref[r, :] = o * gate_ref[r, :]
        return carry

    lax.fori_loop(0, n_chunks, step, 0)
    sto_ref[0, 0] = st_sc[...]


def _retention_tables(chunk):
    lg = jnp.log(1.0 - 2.0 ** (-5.0 - jnp.arange(RET_HEADS, dtype=F32)))
    idx = jnp.arange(chunk, dtype=F32)
    diff = idx[:, None] - idx[None, :]
    decay = jnp.where(diff >= 0, jnp.exp(lg[:, None, None] * jnp.maximum(diff, 0.0)), 0.0)
    gpow = jnp.exp(lg[None, :] * (idx[:, None] + 1.0))
    wts = jnp.exp(lg[:, None] * (chunk - 1.0 - idx)[None, :]).T
    gc = jnp.exp(lg * chunk)
    n_pairs = RET_HEADS // 2
    lanes = lambda t: jnp.repeat(t.reshape(chunk, n_pairs, 2), RET_DV, axis=2).transpose(1, 0, 2)
    i = np.arange(LANES)
    bdm = jnp.asarray(((i[:, None] // RET_DK) == (i[None, :] // RET_DV)).astype(np.float32))
    gc_l = jnp.repeat(gc.reshape(n_pairs, 2), RET_DK, axis=1)
    gc_t = gc_l[:, :, None] * bdm[None]
    return decay.reshape(n_pairs, 2, chunk, chunk), lanes(gpow), lanes(wts), gc_t, bdm


def _retention(q_r, k_r, v_r, gate_r, state_bd, lw, batch, seq):
    chunk = min(RET_CHUNK, seq)
    assert seq % chunk == 0
    n_pairs = RET_HEADS // 2
    dec, gpow, wts, gc_t, bdm = _retention_tables(chunk)
    act = pl.BlockSpec((seq, LANES), lambda b, p: (b, p))
    st = pl.BlockSpec((1, 1, LANES, LANES), lambda b, p: (b, p, 0, 0))
    per_pair = lambda a: pl.BlockSpec((1,) + a.shape[1:], lambda b, p: (p,) + (0,) * (a.ndim - 1))
    const = lambda a: pl.BlockSpec(a.shape, lambda b, p: (0,) * a.ndim)
    return pl.pallas_call(
        _retention_kernel,
        grid=(batch, n_pairs),
        in_specs=[act, act, act, act, st, per_pair(dec), per_pair(gpow), per_pair(wts), per_pair(gc_t),
                  const(bdm), const(lw["g_ret_out"]), const(lw["bd64"])],
        out_specs=[act, st],
        out_shape=[jax.ShapeDtypeStruct((batch * seq, 256), F32),
                   jax.ShapeDtypeStruct((batch, n_pairs, LANES, LANES), F32)],
        scratch_shapes=[pltpu.VMEM((LANES, LANES), F32)],
        compiler_params=pltpu.CompilerParams(dimension_semantics=("arbitrary", "arbitrary"),
                                             vmem_limit_bytes=VMEM_LIMIT),
        name="retention",
    )(q_r, k_r, v_r, gate_r, state_bd, dec, gpow, wts, gc_t, bdm, lw["g_ret_out"], lw["bd64"])


def _state_to_block_diag(st):
    b = st.shape[0]
    s = st.reshape(b, RET_HEADS // 2, 2, RET_DK, RET_DV)
    eye = jnp.eye(2, dtype=st.dtype)
    return jnp.einsum("bpird,ij->bpirjd", s, eye).reshape(b, RET_HEADS // 2, 2 * RET_DK, 2 * RET_DV)


def _block_diag_to_state(bd):
    b = bd.shape[0]
    s = bd.reshape(b, RET_HEADS // 2, 2, RET_DK, 2, RET_DV)
    return jnp.stack([s[:, :, 0, :, 0, :], s[:, :, 1, :, 1, :]], axis=2).reshape(b, RET_HEADS, RET_DK, RET_DV)


def _out_mlp_kernel(x_ref, om_ref, or_ref, ol_ref, wom_ref, wor_ref, wol_ref, gmlp_ref, wup_ref, wdn_ref, y_ref):
    x1 = x_ref[...] + (_dot(_bf(om_ref[...]), wom_ref[...]) + _dot(_bf(or_ref[...]), wor_ref[...])
                       + _dot(_bf(ol_ref[...]), wol_ref[...]))
    h = _bf(_row_rms(x1) * gmlp_ref[...])
    d_ff = wup_ref.shape[1]
    step = min(1024, d_ff)
    acc = x1
    for c in range(d_ff // step):
        u = jnp.maximum(_dot(h, wup_ref[:, c * step:(c + 1) * step]), 0.0)
        acc = acc + _dot(_bf(u * u), wdn_ref[c * step:(c + 1) * step, :])
    y_ref[...] = acc


def _out_mlp(x2d, o_m, o_r, o_l, lw, tile):
    n, d = x2d.shape
    row = lambda w: pl.BlockSpec((tile, w), lambda i: (i, 0))
    const = lambda a: pl.BlockSpec(a.shape, lambda i: (0,) * a.ndim, pipeline_mode=pl.Buffered(1))
    consts = (lw["w_out_m"], lw["w_out_r"], lw["w_out_l"], lw["g_mlp"], lw["w_mlp_up"], lw["w_mlp_down"])
    return pl.pallas_call(
        _out_mlp_kernel,
        grid=(n // tile,),
        in_specs=[row(d), row(384), row(256), row(384)] + [const(a) for a in consts],
        out_specs=row(d),
        out_shape=jax.ShapeDtypeStruct((n, d), F32),
        compiler_params=pltpu.CompilerParams(dimension_semantics=("arbitrary",), vmem_limit_bytes=VMEM_LIMIT),
        name="out_mlp",
    )(x2d, o_m, o_r, o_l, *consts)


def _layer_weights(l, g_attn, w_in, g_moba_q, g_moba_k, g_ret_out, g_mla_qlat, w_mla_q_up, g_mla_qn, g_mla_qp,
                   g_mla_kvlat, w_mla_kv_up, g_mla_kn, g_mla_kp, w_out, g_mlp, w_mlp_up, w_mlp_down):
    ones64 = jnp.ones((64,), F32)
    gains = jnp.stack([
        jnp.tile(g_moba_q[l], 2), jnp.tile(g_moba_k[l], 2), g_mla_kvlat[l], jnp.tile(g_mla_qn[l], 2),
        jnp.concatenate([g_mla_qp[l], g_mla_qp[l], ones64]), jnp.concatenate([g_mla_kp[l], g_mla_kp[l], ones64]),
        jnp.ones((LANES,), F32), jnp.ones((LANES,), F32)])
    kv = w_mla_kv_up[l].reshape(MLA_KV_LORA, MLA_HEADS, MLA_NOPE + MLA_V)
    w_kn = kv[:, :, :MLA_NOPE].reshape(MLA_KV_LORA, MLA_HEADS * MLA_NOPE)
    w_v = kv[:, :, MLA_NOPE:].reshape(MLA_KV_LORA, MLA_HEADS * MLA_V)
    wo = w_out[l]
    perm = [(hk * MOBA_GROUP + g) for g in range(MOBA_GROUP) for hk in range(MOBA_KV_HEADS)]
    wo_m = wo[:MOBA_HEADS * HEAD_DIM].reshape(MOBA_HEADS, HEAD_DIM, -1)[jnp.asarray(perm)].reshape(MOBA_HEADS * HEAD_DIM, -1)
    r0 = MOBA_HEADS * HEAD_DIM
    r1 = r0 + RET_HEADS * RET_DV
    return {
        "g_attn": g_attn[l][None, :],
        "w_in": _bf(_take_cols(w_in[l], _in_proj_columns())),
        "w_q_up": _bf(_take_cols(w_mla_q_up[l], _q_up_columns())),
        "gains": gains,
        "g_qlat": g_mla_qlat[l][None, :],
        "bd64": _block_diag_avg(64),
        "bd32": _block_diag_avg(32),
        "w_kv_up": _bf(jnp.concatenate([w_kn, w_v], axis=1)),
        "w_kn_f32": w_kn,
        "w_v": _bf(w_v),
        "g_kn": jnp.tile(g_mla_kn[l], 2)[None, :],
        "g_ret_out": jnp.tile(g_ret_out[l], 2)[None, :],
        "w_out_m": _bf(wo_m), "w_out_r": _bf(wo[r0:r1]), "w_out_l": _bf(wo[r1:]),
        "g_mlp": g_mlp[l][None, :],
        "w_mlp_up": _bf(w_mlp_up[l]), "w_mlp_down": _bf(w_mlp_down[l]),
    }


def _rows_of_page(page):
    return lambda buf, slot, pg: buf.at[slot, pl.ds(pg * page, page), :]


def _whole_page(buf, slot, pg):
    return buf.at[slot, pg]


def _page_copies(pt_ref, layer, req, chunk, slot, pages_per_chunk, streams):
    copies = []
    for hbm, buf, sem, window in streams:
        for pg in range(pages_per_chunk):
            pid = pt_ref[req, chunk * pages_per_chunk + pg]
            copies.append(pltpu.make_async_copy(hbm.at[layer, pid], window(buf, slot, pg), sem.at[slot]))
    return copies


def _gather_step(pt_ref, layer, pages_per_chunk, streams):
    r, ch = pl.program_id(0), pl.program_id(1)
    n_req, n_ch = pl.num_programs(0), pl.num_programs(1)
    step = r * n_ch + ch
    slot = step % 2

    @pl.when(step == 0)
    def _():
        for c in _page_copies(pt_ref, layer, 0, 0, 0, pages_per_chunk, streams):
            c.start()

    @pl.when(step + 1 < n_req * n_ch)
    def _():
        wrap = ch + 1 == n_ch
        nr = jnp.where(wrap, r + 1, r)
        nc = jnp.where(wrap, 0, ch + 1)
        for c in _page_copies(pt_ref, layer, nr, nc, 1 - slot, pages_per_chunk, streams):
            c.start()

    for c in _page_copies(pt_ref, layer, r, ch, slot, pages_per_chunk, streams):
        c.wait()
    return slot


def _mla_decode_kernel(layer, pages_per_chunk, page, sub,
                       pt_ref, lat_hbm, kpe_hbm, qn_ref, qpe_ref, cnew_ref, penew_ref, wkt_ref, gkn_ref, e_ref,
                       wv_ref, o_ref, cbuf, pbuf, sem_c, sem_p, lhs_sc, m_sc, l_sc, acc_sc):
    ch, n_ch = pl.program_id(1), pl.num_programs(1)
    n_dec = cnew_ref.shape[0]
    rows = MLA_HEADS * n_dec
    n_y = MLA_HEADS * MLA_NOPE
    slot = _gather_step(pt_ref, layer, pages_per_chunk,
                        ((lat_hbm, cbuf, sem_c, _rows_of_page(page)), (kpe_hbm, pbuf, sem_p, _whole_page)))

    @pl.when(ch == 0)
    def _():
        lhs_sc[0:n_y, :] = wkt_ref[...]
        for h in range(MLA_HEADS):
            qn = qn_ref[0, h * n_dec:(h + 1) * n_dec, :] * gkn_ref[...]
            lhs_sc[n_y + h * n_dec:n_y + (h + 1) * n_dec, :] = _dot(
                _bf(qn), _bf(wkt_ref[h * MLA_NOPE:(h + 1) * MLA_NOPE, :]))
        m_sc[...] = jnp.full(m_sc.shape, NEG_INF, F32)
        l_sc[...] = jnp.zeros(l_sc.shape, F32)
        acc_sc[...] = jnp.zeros(acc_sc.shape, F32)

    lhs = _bf(lhs_sc[...])
    qpe = _bf(qpe_ref[0])
    e = e_ref[...]
    scale = MLA_QK ** -0.5

    def scores(cb, s_pe):
        r = _dot_nt(lhs, cb)
        y = r[0:n_y, :]
        hi, lo = _split_bf16(y * y)
        rinv = lax.rsqrt(_dot(e, hi) + _dot(e, lo) + EPS)
        return (r[n_y:n_y + rows, :] * rinv + s_pe) * scale

    def update(s, cb):
        m_old = m_sc[...]
        m_new = jnp.maximum(m_old, jnp.max(s, axis=-1, keepdims=True))
        alpha = jnp.exp(m_old - m_new)
        p = jnp.exp(s - m_new)
        m_sc[...] = m_new
        l_sc[...] = alpha * l_sc[...] + jnp.sum(p, axis=-1, keepdims=True)
        acc_sc[...] = alpha * acc_sc[...] + _dot(_bf(p), cb)

    pages_per_sub = sub // page

    def tile(t, carry):
        rws = pl.ds(pl.multiple_of(t * sub, sub), sub)
        cb = _bf(cbuf[slot, rws, :])
        s_pe = jnp.concatenate([_dot(qpe, _bf(pbuf[slot, t * pages_per_sub + i])) for i in range(pages_per_sub)],
                               axis=1)
        update(scores(cb, s_pe), cb)
        return carry

    lax.fori_loop(0, (pages_per_chunk * page) // sub, tile, 0)

    @pl.when(ch == n_ch - 1)
    def _():
        cb = _bf(cnew_ref[...])
        s = scores(cb, _dot_nt(qpe, _bf(penew_ref[0])))
        t_row = lax.broadcasted_iota(jnp.int32, s.shape, 0) % n_dec
        update(jnp.where(_lane(s.shape) <= t_row, s, NEG_INF), cb)
        o_lat = _bf(acc_sc[...] / l_sc[...])
        lane = _lane((n_dec, LANES))
        for c in range(MLA_HEADS // 2):
            wv = wv_ref[:, c * LANES:(c + 1) * LANES]
            o0 = _dot(o_lat[(2 * c) * n_dec:(2 * c + 1) * n_dec, :], wv)
            o1 = _dot(o_lat[(2 * c + 1) * n_dec:(2 * c + 2) * n_dec, :], wv)
            o_ref[:, c * LANES:(c + 1) * LANES] = jnp.where(lane < MLA_V, o0, o1)


def _decode_chunk_keys(past_len, page):
    keys = min(2048, max(page, past_len // 2))
    assert past_len % keys == 0 and keys % page == 0, (past_len, keys, page)
    return keys


def _mla_decode(layer, page_table, lat_cache, kpe_cache, q_l, c_new, kpe_slot, lw, n_req, n_dec):
    page = lat_cache.shape[2]
    past_len = page_table.shape[1] * page
    keys = _decode_chunk_keys(past_len, page)
    sub = min(512, keys)
    rows = MLA_HEADS * n_dec
    arr = q_l.reshape(n_req, n_dec, MLA_HEADS // 2, 256)
    heads_first = lambda a, w: a.reshape(n_req, n_dec, MLA_HEADS // 2, 2, w).transpose(0, 2, 3, 1, 4).reshape(n_req, rows, w)
    qn_rows = heads_first(arr[..., :LANES], MLA_NOPE)
    qpe_rows = heads_first(arr[..., LANES:LANES + 2 * MLA_ROPE], MLA_ROPE)
    pe_new = kpe_slot[:, :MLA_ROPE].reshape(n_req, n_dec, MLA_ROPE)
    w_kn_t = lw["w_kn_f32"].T
    i = np.arange(rows)[:, None] // n_dec
    j = np.arange(MLA_HEADS * MLA_NOPE)[None, :] // MLA_NOPE
    e = jnp.asarray((i == j).astype(np.float32) / MLA_NOPE, BF16)
    g_kn = lw["g_kn"][:, :MLA_NOPE]
    const = lambda a: pl.BlockSpec(a.shape, lambda r, c, pt: (0,) * a.ndim)
    kern = functools.partial(_mla_decode_kernel, layer, keys // page, page, sub)
    return pl.pallas_call(
        kern,
        grid_spec=pltpu.PrefetchScalarGridSpec(
            num_scalar_prefetch=1,
            grid=(n_req, past_len // keys),
            in_specs=[pl.BlockSpec(memory_space=pl.ANY), pl.BlockSpec(memory_space=pl.ANY),
                      pl.BlockSpec((1, rows, MLA_NOPE), lambda r, c, pt: (r, 0, 0)),
                      pl.BlockSpec((1, rows, MLA_ROPE), lambda r, c, pt: (r, 0, 0)),
                      pl.BlockSpec((n_dec, LANES), lambda r, c, pt: (r, 0)),
                      pl.BlockSpec((1, n_dec, MLA_ROPE), lambda r, c, pt: (r, 0, 0)),
                      const(w_kn_t), const(g_kn), const(e), const(lw["w_v"])],
            out_specs=pl.BlockSpec((n_dec, MLA_HEADS * MLA_V), lambda r, c, pt: (r, 0)),
            scratch_shapes=[pltpu.VMEM((2, keys, LANES), F32), pltpu.VMEM((2, keys // page, MLA_ROPE, page), F32),
                            pltpu.SemaphoreType.DMA((2,)), pltpu.SemaphoreType.DMA((2,)),
                            pltpu.VMEM((MLA_HEADS * MLA_NOPE + rows, LANES), F32),
                            pltpu.VMEM((rows, 1), F32), pltpu.VMEM((rows, 1), F32), pltpu.VMEM((rows, LANES), F32)]),
        out_shape=jax.ShapeDtypeStruct((n_req * n_dec, MLA_HEADS * MLA_V), F32),
        compiler_params=pltpu.CompilerParams(dimension_semantics=("arbitrary", "arbitrary"),
                                             vmem_limit_bytes=VMEM_LIMIT),
        name="mla_decode",
    )(page_table, lat_cache, kpe_cache.transpose(0, 1, 3, 2), qn_rows, qpe_rows, c_new, pe_new, w_kn_t, g_kn, e,
      lw["w_v"])


def _moba_decode_kernel(layer, pages_per_chunk, page,
                        pt_ref, k_hbm, v_hbm, q_ref, knew_ref, vnew_ref, o_ref,
                        kbuf, vbuf, sem_k, sem_v, vall_sc, s_sc, kmean_sc, sel_sc, m_sc, l_sc, acc_sc):
    ch, n_ch = pl.program_id(1), pl.num_programs(1)
    n_dec = knew_ref.shape[0]
    keys = pages_per_chunk * page
    blocks_per_chunk = keys // MOBA_BLOCK
    slot = _gather_step(pt_ref, layer, pages_per_chunk,
                        ((k_hbm, kbuf, sem_k, _whole_page), (v_hbm, vbuf, sem_v, _whole_page)))
    q = q_ref[0]
    qb = _bf(q * (HEAD_DIM ** -0.5))
    n_blk = kmean_sc.shape[1]
    pages_per_block = MOBA_BLOCK // page

    @pl.when(ch == 0)
    def _():
        kmean_sc[...] = jnp.zeros(kmean_sc.shape, F32)

    blk_col = _lane((page, n_blk))
    for b in range(blocks_per_chunk):
        blk = ch * blocks_per_chunk + b
        ksum = None
        for i in range(pages_per_block):
            pg = b * pages_per_block + i
            kt = kbuf[slot, pg].reshape(LANES, page)
            ksum = kt if ksum is None else ksum + kt
            s_sc[blk, :, i * page:(i + 1) * page] = _dot(qb, _bf(kt))
            vall_sc[blk * pages_per_block + i] = _bf(vbuf[slot, pg].reshape(LANES, page))
        hi, lo = _split_bf16(ksum)
        pick = _bf(jnp.where(blk_col == blk, 1.0 / MOBA_BLOCK, 0.0))
        kmean_sc[...] += _dot(hi, pick) + _dot(lo, pick)

    @pl.when(ch == n_ch - 1)
    def _():
        sel_sc[...] = _topk_mask(_dot_f32(q, kmean_sc[...]), n_blk, MOBA_TOPK)
        s = _dot_nt(qb, _bf(knew_ref[...]))
        t_row = lax.broadcasted_iota(jnp.int32, s.shape, 0) % n_dec
        s = jnp.where(_lane(s.shape) <= t_row, s, NEG_INF)
        m = jnp.max(s, axis=-1, keepdims=True)
        p = jnp.exp(s - m)
        m_sc[...] = m
        l_sc[...] = jnp.sum(p, axis=-1, keepdims=True)
        acc_sc[...] = _dot(_bf(p), _bf(vnew_ref[...]))

        def past(j, carry):
            sel = sel_sc[...]
            chosen = jnp.sum(jnp.where(_lane(sel.shape) == j, sel, 0.0), axis=-1, keepdims=True) > 0.5
            s = jnp.where(chosen, s_sc[j], NEG_INF)
            m_old = m_sc[...]
            m_new = jnp.maximum(m_old, jnp.max(s, axis=-1, keepdims=True))
            alpha = jnp.exp(m_old - m_new)
            p = jnp.exp(s - m_new)
            m_sc[...] = m_new
            l_sc[...] = alpha * l_sc[...] + jnp.sum(p, axis=-1, keepdims=True)
            pb = _bf(p)
            pv = None
            for i in range(pages_per_block):
                part = _dot_nt(pb[:, i * page:(i + 1) * page], vall_sc[j * pages_per_block + i])
                pv = part if pv is None else pv + part
            acc_sc[...] = alpha * acc_sc[...] + pv
            return carry

        lax.fori_loop(0, n_blk, past, 0)
        o_ref[0] = acc_sc[...] / l_sc[...]


def _moba_decode(layer, page_table, k_cache, v_cache, q_m, k_new, v_new, n_req, n_dec):
    page = k_cache.shape[2]
    past_len = page_table.shape[1] * page
    assert past_len % MOBA_BLOCK == 0 and MOBA_BLOCK % page == 0
    keys = _decode_chunk_keys(past_len, page)
    assert keys % MOBA_BLOCK == 0
    n_blk = past_len // MOBA_BLOCK
    rows = MOBA_HEADS * n_dec
    qg = q_m.reshape(n_req, n_dec, MOBA_GROUP, LANES).transpose(0, 2, 1, 3)
    half = (np.arange(LANES) // HEAD_DIM)[None, :] == np.arange(MOBA_KV_HEADS)[:, None]
    q_rows = (qg[:, None] * jnp.asarray(half.astype(np.float32))[None, :, None, None, :]).reshape(n_req, rows, LANES)
    k_t = k_cache.transpose(0, 1, 3, 4, 2)
    v_t = v_cache.transpose(0, 1, 3, 4, 2)
    page_buf = pltpu.VMEM((2, keys // page, MOBA_KV_HEADS, HEAD_DIM, page), F32)
    kern = functools.partial(_moba_decode_kernel, layer, keys // page, page)
    o_rows = pl.pallas_call(
        kern,
        grid_spec=pltpu.PrefetchScalarGridSpec(
            num_scalar_prefetch=1,
            grid=(n_req, past_len // keys),
            in_specs=[pl.BlockSpec(memory_space=pl.ANY), pl.BlockSpec(memory_space=pl.ANY),
                      pl.BlockSpec((1, rows, LANES), lambda r, c, pt: (r, 0, 0)),
                      pl.BlockSpec((n_dec, LANES), lambda r, c, pt: (r, 0)),
                      pl.BlockSpec((n_dec, LANES), lambda r, c, pt: (r, 0))],
            out_specs=pl.BlockSpec((1, rows, LANES), lambda r, c, pt: (r, 0, 0)),
            scratch_shapes=[page_buf, page_buf,
                            pltpu.SemaphoreType.DMA((2,)), pltpu.SemaphoreType.DMA((2,)),
                            pltpu.VMEM((past_len // page, LANES, page), BF16),
                            pltpu.VMEM((n_blk, rows, MOBA_BLOCK), F32),
                            pltpu.VMEM((LANES, n_blk), F32), pltpu.VMEM((rows, n_blk), F32),
                            pltpu.VMEM((rows, 1), F32), pltpu.VMEM((rows, 1), F32), pltpu.VMEM((rows, LANES), F32)]),
        out_shape=jax.ShapeDtypeStruct((n_req, rows, LANES), F32),
        compiler_params=pltpu.CompilerParams(dimension_semantics=("arbitrary", "arbitrary"),
                                             vmem_limit_bytes=VMEM_LIMIT),
        name="moba_decode",
    )(page_table, k_t, v_t, q_rows, k_new, v_new)
    o = o_rows.reshape(n_req, MOBA_KV_HEADS, MOBA_GROUP, n_dec, LANES)
    pairs = jnp.concatenate([o[:, 0, :, :, :HEAD_DIM], o[:, 1, :, :, HEAD_DIM:]], axis=-1)
    return pairs.transpose(0, 2, 1, 3).reshape(n_req * n_dec, MOBA_GROUP * LANES)


def _layer_sample(layer, x2d, lw, caches, state, page_table, n_req, n_dec):
    ck, cv, clat, ckpe = caches
    past_len = page_table.shape[1] * ck.shape[2]
    n = n_req * n_dec
    tile = _tile_rows(n, 256)
    assert tile % n_dec == 0
    pos = jnp.tile(past_len + jnp.arange(n_dec), tile // n_dec)
    q_m, k_m, v_m, q_r, k_r, v_r, gate_r, q_l, c_kv, kpe = _project(x2d, pos, 1, lw, tile)
    o_m = _moba_decode(layer, page_table, ck, cv, q_m, k_m, v_m, n_req, n_dec)
    o_r, st = _retention(q_r, k_r, v_r, gate_r, _state_to_block_diag(state.astype(F32)), lw, n_req, n_dec)
    o_l = _mla_decode(layer, page_table, clat, ckpe, q_l, c_kv, kpe, lw, n_req, n_dec)
    y = _out_mlp(x2d, o_m, o_r, o_l, lw, _tile_rows(n, 512))
    new = (k_m.reshape(n_req, n_dec, MOBA_KV_HEADS, HEAD_DIM), v_m.reshape(n_req, n_dec, MOBA_KV_HEADS, HEAD_DIM),
           c_kv.reshape(n_req, n_dec, MLA_KV_LORA), kpe[:, :MLA_ROPE].reshape(n_req, n_dec, MLA_ROPE),
           _block_diag_to_state(st).astype(state.dtype))
    return y, new


def _layer_prompt(x2d, lw, batch, seq):
    tile = _tile_rows(seq, 256)
    q_m, k_m, v_m, q_r, k_r, v_r, gate_r, q_l, c_kv, kpe = _project(
        x2d, jnp.arange(seq), seq // tile, lw, tile)
    o_m = _moba_prompt(q_m, k_m, v_m, batch, seq)
    st0 = jnp.zeros((batch, RET_HEADS // 2, LANES, LANES), F32)
    o_r, st = _retention(q_r, k_r, v_r, gate_r, st0, lw, batch, seq)
    o_l = _mla_prompt(q_l, c_kv, kpe, lw, batch, seq)
    y = _out_mlp(x2d, o_m, o_r, o_l, lw, _tile_rows(x2d.shape[0], 512))
    new = (k_m.reshape(batch, seq, MOBA_KV_HEADS, HEAD_DIM), v_m.reshape(batch, seq, MOBA_KV_HEADS, HEAD_DIM),
           c_kv.reshape(batch, seq, MLA_KV_LORA), kpe[:, :MLA_ROPE].reshape(batch, seq, MLA_ROPE),
           _block_diag_to_state(st))
    return y, new


def kernel(x_prompt, x_sample, cache_moba_k, cache_moba_v, cache_mla_latent, cache_mla_kpe, state_ret, page_table,
           g_attn, w_in, g_moba_q, g_moba_k, g_ret_out, g_mla_qlat, w_mla_q_up, g_mla_qn, g_mla_qp, g_mla_kvlat,
           w_mla_kv_up, g_mla_kn, g_mla_kp, w_out, g_mlp, w_mlp_up, w_mlp_down):
    batch, seq, d_model = x_prompt.shape
    n_req, n_dec, _ = x_sample.shape
    depth = w_in.shape[0]
    params = (g_attn, w_in, g_moba_q, g_moba_k, g_ret_out, g_mla_qlat, w_mla_q_up, g_mla_qn, g_mla_qp,
              g_mla_kvlat, w_mla_kv_up, g_mla_kn, g_mla_kp, w_out, g_mlp, w_mlp_up, w_mlp_down)
    yp = x_prompt.reshape(batch * seq, d_model)
    ys = x_sample.reshape(n_req * n_dec, d_model)
    caches = (cache_moba_k, cache_moba_v, cache_mla_latent, cache_mla_kpe)
    new_p, new_s = [], []
    for l in range(depth):
        lw = _layer_weights(l, *params)
        yp, st_p = _layer_prompt(yp, lw, batch, seq)
        ys, st_s = _layer_sample(l, ys, lw, caches, state_ret[l], page_table, n_req, n_dec)
        new_p.append(st_p)
        new_s.append(st_s)
    stacked_p = [jnp.stack(a) for a in zip(*new_p)]
    stacked_s = [jnp.stack(a) for a in zip(*new_s)]
    return (yp.reshape(batch, seq, d_model), ys.reshape(n_req, n_dec, d_model), *stacked_p, *stacked_s)
```

```python
import functools

import numpy as np
import jax
import jax.numpy as jnp
from jax import lax
from jax.experimental import pallas as pl
from jax.experimental.pallas import tpu as pltpu

HEAD_DIM = 64
MOBA_HEADS = 6
MOBA_KV_HEADS = 2
MOBA_GROUP = MOBA_HEADS // MOBA_KV_HEADS
MOBA_BLOCK = 256
MOBA_TOPK = 3
RET_HEADS = 4
RET_DK = 64
RET_DV = 64
RET_CHUNK = 128
MLA_HEADS = 6
MLA_Q_LORA = 256
MLA_KV_LORA = 128
MLA_NOPE = 64
MLA_ROPE = 32
MLA_V = 64
MLA_QK = MLA_NOPE + MLA_ROPE
ROPE_THETA = 10000.0
EPS = 1e-6

LANES = 128
VMEM_LIMIT = 56 * 1024 * 1024
F32 = jnp.float32
BF16 = jnp.bfloat16
NEG_INF = float("-inf")

N_IN_CHUNKS = 17


def _bf(x):
    return x.astype(BF16)


def _dot(a, b):
    return jnp.dot(a, b, preferred_element_type=F32)


def _dot_nt(a, b):
    return lax.dot_general(a, b, (((1,), (1,)), ((), ())), preferred_element_type=F32)


def _split_bf16(x):
    hi = _bf(x)
    lo = _bf(x - hi.astype(F32))
    return hi, lo


def _dot_nt_f32(a, b):
    ah, al = _split_bf16(a)
    bh, bl = _split_bf16(b)
    return _dot_nt(ah, bh) + (_dot_nt(ah, bl) + _dot_nt(al, bh))


def _dot_f32(a, b):
    ah, al = _split_bf16(a)
    bh, bl = _split_bf16(b)
    return _dot(ah, bh) + (_dot(ah, bl) + _dot(al, bh))


def _group_mean_sq(x, bd):
    hi, lo = _split_bf16(x * x)
    return _dot(hi, bd) + _dot(lo, bd)


def _lane(shape):
    return lax.broadcasted_iota(jnp.int32, shape, len(shape) - 1)


def _rotate_half(x, half):
    n = x.shape[-1]
    up = pltpu.roll(x, n - half, x.ndim - 1)
    down = pltpu.roll(x, half, x.ndim - 1)
    return jnp.where(_lane(x.shape) % (2 * half) < half, up, down)


def _rope(x, cos, sin_signed, half):
    return x * cos + _rotate_half(x, half) * sin_signed


def _row_rms(x):
    return x * lax.rsqrt(jnp.mean(x * x, axis=-1, keepdims=True) + EPS)


def _proj_kernel(x_ref, gattn_ref, win_ref, wq_ref, gains_ref, gqlat_ref, bd64_ref, bd32_ref,
                 cos64_ref, sin64_ref, cos32_ref, sin32_ref,
                 qm_ref, km_ref, vm_ref, qr_ref, kr_ref, vr_ref, gr_ref, ql_ref, ckv_ref, kpe_ref):
    x = x_ref[...]
    h = _row_rms(x) * gattn_ref[...]
    z = _dot(_bf(h), win_ref[...])
    bd64 = bd64_ref[...]
    bd32 = bd32_ref[...]
    cos64, sin64 = cos64_ref[...], sin64_ref[...]
    cos32, sin32 = cos32_ref[...], sin32_ref[...]
    g_mq, g_mk = gains_ref[0:1, :], gains_ref[1:2, :]
    g_kvlat, g_qn = gains_ref[2:3, :], gains_ref[3:4, :]
    g_qp, g_kp = gains_ref[4:5, :], gains_ref[5:6, :]

    def chunk(i):
        return z[:, i * LANES:(i + 1) * LANES]

    def norm_group(v, bd, g):
        return v * lax.rsqrt(_group_mean_sq(v, bd) + EPS) * g

    for i in range(3):
        qm_ref[:, i * LANES:(i + 1) * LANES] = _rope(norm_group(chunk(i), bd64, g_mq), cos64, sin64, 32)
    km_ref[...] = _rope(norm_group(chunk(3), bd64, g_mk), cos64, sin64, 32)
    vm_ref[...] = chunk(4)
    for i in range(2):
        qr_ref[:, i * LANES:(i + 1) * LANES] = _rope(chunk(5 + i), cos64, sin64, 32)
        kr_ref[:, i * LANES:(i + 1) * LANES] = _rope(chunk(7 + i), cos64, sin64, 32) * (RET_DK ** -0.5)
        vr_ref[:, i * LANES:(i + 1) * LANES] = chunk(9 + i)
        g = chunk(11 + i)
        gr_ref[:, i * LANES:(i + 1) * LANES] = g * jax.nn.sigmoid(g)
    zq = z[:, 13 * LANES:15 * LANES]
    ql = _dot(_bf(_row_rms(zq) * gqlat_ref[...]), wq_ref[...])
    for c in range(3):
        nope = ql[:, c * 256:c * 256 + LANES]
        pe = ql[:, c * 256 + LANES:(c + 1) * 256]
        ql_ref[:, c * 256:c * 256 + LANES] = norm_group(nope, bd64, g_qn)
        ql_ref[:, c * 256 + LANES:(c + 1) * 256] = _rope(norm_group(pe, bd32, g_qp), cos32, sin32, 16)
    ckv_ref[...] = _row_rms(chunk(15)) * g_kvlat
    kpe_ref[...] = _rope(norm_group(chunk(16), bd32, g_kp), cos32, sin32, 16)


def _in_proj_columns():
    cols = []
    for g in range(MOBA_GROUP):
        for hk in range(MOBA_KV_HEADS):
            h = hk * MOBA_GROUP + g
            cols += list(range(h * HEAD_DIM, (h + 1) * HEAD_DIM))
    cols += list(range(384, 2048))
    cols += list(range(2048, 2080)) * 2 + [-1] * 64
    return np.asarray(cols, np.int32)


def _q_up_columns():
    cols = []
    for c in range(MLA_HEADS // 2):
        for h in (2 * c, 2 * c + 1):
            cols += list(range(h * MLA_QK, h * MLA_QK + MLA_NOPE))
        for h in (2 * c, 2 * c + 1):
            cols += list(range(h * MLA_QK + MLA_NOPE, (h + 1) * MLA_QK))
        cols += [-1] * 64
    return np.asarray(cols, np.int32)


def _take_cols(w, cols):
    out = jnp.take(w, jnp.asarray(np.maximum(cols, 0)), axis=1)
    return out * jnp.asarray((cols >= 0).astype(np.float32))[None, :]


def _block_diag_avg(group):
    i = np.arange(LANES)
    return jnp.asarray(((i[:, None] // group) == (i[None, :] // group)).astype(np.float32) / group, BF16)


def _rope_tables(pos, dim):
    half = dim // 2
    inv = ROPE_THETA ** (-jnp.arange(half, dtype=F32) / half)
    ang = pos.astype(F32)[:, None] * inv[None, :]
    cos, sin = jnp.cos(ang), jnp.sin(ang)
    reps = LANES // dim
    return jnp.tile(jnp.concatenate([cos, cos], -1), (1, reps)), jnp.tile(jnp.concatenate([-sin, sin], -1), (1, reps))


def _tile_rows(n_rows, cap):
    t = min(cap, n_rows)
    assert n_rows % t == 0 and t % 8 == 0, (n_rows, t)
    return t


def _project(x2d, pos, n_pos_tiles, lw, tile):
    n, d = x2d.shape
    c64, s64 = _rope_tables(pos, 64)
    c32, s32 = _rope_tables(pos, 32)
    grid = (n // tile,)
    row = lambda w: pl.BlockSpec((tile, w), lambda i: (i, 0))
    const = lambda a: pl.BlockSpec(a.shape, lambda i: (0,) * a.ndim)
    tab = pl.BlockSpec((tile, LANES), lambda i: (i % n_pos_tiles, 0))
    widths = (384, 128, 128, 256, 256, 256, 256, 768, 128, 128)
    consts = (lw["g_attn"], lw["w_in"], lw["w_q_up"], lw["gains"], lw["g_qlat"], lw["bd64"], lw["bd32"])
    return pl.pallas_call(
        _proj_kernel,
        grid=grid,
        in_specs=[row(d)] + [const(a) for a in consts] + [tab] * 4,
        out_specs=[row(w) for w in widths],
        out_shape=[jax.ShapeDtypeStruct((n, w), F32) for w in widths],
        compiler_params=pltpu.CompilerParams(dimension_semantics=("arbitrary",), vmem_limit_bytes=VMEM_LIMIT),
        name="in_proj",
    )(x2d, *consts, c64, s64, c32, s32)


def _topk_mask(gate, n_valid, k):
    col = _lane(gate.shape)
    g = jnp.where(col < n_valid, gate, NEG_INF)
    sel = jnp.zeros(gate.shape, F32)
    big = jnp.int32(gate.shape[-1])
    for _ in range(k):
        m = jnp.max(g, axis=-1, keepdims=True)
        idx = jnp.min(jnp.where(g == m, col, big), axis=-1, keepdims=True)
        pick = (col == idx) & (m > NEG_INF)
        sel = jnp.where(pick, 1.0, sel)
        g = jnp.where(pick, NEG_INF, g)
    return sel


def _topk_mask_t(gate, n_valid, k):
    row = lax.broadcasted_iota(jnp.int32, gate.shape, 0)
    g = jnp.where(row < n_valid, gate, NEG_INF)
    sel = jnp.zeros(gate.shape, F32)
    big = jnp.int32(gate.shape[0])
    for _ in range(k):
        m = jnp.max(g, axis=0, keepdims=True)
        idx = jnp.min(jnp.where(g == m, row, big), axis=0, keepdims=True)
        pick = (row == idx) & (m > NEG_INF)
        sel = jnp.where(pick, 1.0, sel)
        g = jnp.where(pick, NEG_INF, g)
    return sel


def _moba_prompt_kernel(q_ref, k_ref, v_ref, o_ref, kb_sc, vt_sc, kmean_sc, qf_sc, qs_sc, sel_sc, m_sc, l_sc, acc_sc):
    tq = q_ref.shape[0]
    n_blk = kmean_sc.shape[0]
    qi = pl.program_id(1)

    @pl.when(qi == 0)
    def _():
        kb_sc[...] = _bf(k_ref[...])
        for j in range(n_blk):
            blk = slice(j * MOBA_BLOCK, (j + 1) * MOBA_BLOCK)
            vt_sc[j] = _bf(v_ref[blk, :].T)
            kmean_sc[j:j + 1, :] = jnp.mean(k_ref[blk, :], axis=0, keepdims=True)

    dim_row = lax.broadcasted_iota(jnp.int32, (LANES, tq), 0)
    for g in range(MOBA_GROUP):
        chunk_t = q_ref[:, g * LANES:(g + 1) * LANES].T
        for hk in range(MOBA_KV_HEADS):
            s = g * MOBA_KV_HEADS + hk
            qf_sc[:, s * tq:(s + 1) * tq] = jnp.where((dim_row // HEAD_DIM) == hk, chunk_t, 0.0)
    q_t = qf_sc[...]
    sel_sc[...] = _topk_mask_t(_dot_f32(kmean_sc[...], q_t), qi, MOBA_TOPK)
    qs_sc[...] = _bf(q_t * (HEAD_DIM ** -0.5))

    own = pl.multiple_of(qi * MOBA_BLOCK, MOBA_BLOCK)
    s = _dot(kb_sc[pl.ds(own, MOBA_BLOCK), :], qs_sc[...])
    kpos = lax.broadcasted_iota(jnp.int32, s.shape, 0)
    s = jnp.where(kpos <= _lane(s.shape) % tq, s, NEG_INF)
    m = jnp.max(s, axis=0, keepdims=True)
    p = jnp.exp(s - m)
    m_sc[...] = m
    l_sc[...] = jnp.sum(p, axis=0, keepdims=True)
    acc_sc[...] = _dot(vt_sc[qi], _bf(p))

    def past(j, carry):
        start = pl.multiple_of(j * MOBA_BLOCK, MOBA_BLOCK)
        chosen = sel_sc[pl.ds(j, 1), :] > 0.5
        s = jnp.where(chosen, _dot(kb_sc[pl.ds(start, MOBA_BLOCK), :], qs_sc[...]), NEG_INF)
        m_old = m_sc[...]
        m_new = jnp.maximum(m_old, jnp.max(s, axis=0, keepdims=True))
        alpha = jnp.exp(m_old - m_new)
        p = jnp.exp(s - m_new)
        m_sc[...] = m_new
        l_sc[...] = alpha * l_sc[...] + jnp.sum(p, axis=0, keepdims=True)
        acc_sc[...] = alpha * acc_sc[...] + _dot(vt_sc[j], _bf(p))
        return carry

    lax.fori_loop(0, qi, past, 0)

    o_t = acc_sc[...] / l_sc[...]
    for g in range(MOBA_GROUP):
        c0, c1 = (2 * g) * tq, (2 * g + 1) * tq
        pair_t = jnp.where(dim_row < HEAD_DIM, o_t[:, c0:c0 + tq], o_t[:, c1:c1 + tq])
        o_ref[:, g * LANES:(g + 1) * LANES] = pair_t.T


def _moba_prompt(q_m, k_m, v_m, batch, seq):
    assert seq % MOBA_BLOCK == 0
    tq = MOBA_BLOCK
    n_blk = seq // MOBA_BLOCK
    nq = seq // tq
    rows = MOBA_HEADS * tq
    return pl.pallas_call(
        _moba_prompt_kernel,
        grid=(batch, nq),
        in_specs=[pl.BlockSpec((tq, 384), lambda b, i: (b * nq + i, 0)),
                  pl.BlockSpec((seq, LANES), lambda b, i: (b, 0)),
                  pl.BlockSpec((seq, LANES), lambda b, i: (b, 0))],
        out_specs=pl.BlockSpec((tq, 384), lambda b, i: (b * nq + i, 0)),
        out_shape=jax.ShapeDtypeStruct((batch * seq, 384), F32),
        scratch_shapes=[pltpu.VMEM((seq, LANES), BF16), pltpu.VMEM((n_blk, LANES, MOBA_BLOCK), BF16),
                        pltpu.VMEM((n_blk, LANES), F32), pltpu.VMEM((LANES, rows), F32),
                        pltpu.VMEM((LANES, rows), BF16), pltpu.VMEM((n_blk, rows), F32),
                        pltpu.VMEM((1, rows), F32), pltpu.VMEM((1, rows), F32), pltpu.VMEM((LANES, rows), F32)],
        compiler_params=pltpu.CompilerParams(dimension_semantics=("arbitrary", "arbitrary"),
                                             vmem_limit_bytes=VMEM_LIMIT),
        name="moba_prompt",
    )(q_m, k_m, v_m)


def _mla_prompt_kernel(q_ref, ckv_ref, kpe_ref, wkv_ref, gkn_ref, bd64_ref, o_ref,
                       kcat_sc, vt_sc, qs_sc, m_sc, l_sc, acc_sc):
    tq = q_ref.shape[0]
    seq = ckv_ref.shape[0]
    n_pairs = MLA_HEADS // 2
    qi = pl.program_id(1)

    @pl.when(qi == 0)
    def _():
        def expand(t, carry):
            r = pl.ds(pl.multiple_of(t * tq, tq), tq)
            kv = _dot(_bf(ckv_ref[r, :]), wkv_ref[...])
            pe = _bf(kpe_ref[r, :])
            for c in range(n_pairs):
                kn = kv[:, c * LANES:(c + 1) * LANES]
                kn = kn * lax.rsqrt(_group_mean_sq(kn, bd64_ref[...]) + EPS) * gkn_ref[...]
                kcat_sc[c, r, 0:LANES] = _bf(kn)
                kcat_sc[c, r, LANES:2 * LANES] = pe
                vt_sc[c, t] = _bf(kv[:, (n_pairs + c) * LANES:(n_pairs + c + 1) * LANES].T)
            return carry
        lax.fori_loop(0, seq // tq, expand, 0)

    depth = lax.broadcasted_iota(jnp.int32, (256, tq), 0)
    for c in range(n_pairs):
        chunk_t = (q_ref[:, c * 256:(c + 1) * 256] * (MLA_QK ** -0.5)).T
        for hh in range(2):
            nope = (depth < LANES) & ((depth // MLA_NOPE) == hh)
            pe = (depth >= LANES) & (depth < LANES + 2 * MLA_ROPE) & (((depth - LANES) // MLA_ROPE) == hh)
            qs_sc[c, :, hh * tq:(hh + 1) * tq] = _bf(jnp.where(nope | pe, chunk_t, 0.0))

    own = pl.multiple_of(qi * tq, tq)
    for c in range(n_pairs):
        s = _dot(kcat_sc[c, pl.ds(own, tq), :], qs_sc[c])
        kpos = lax.broadcasted_iota(jnp.int32, s.shape, 0)
        s = jnp.where(kpos <= _lane(s.shape) % tq, s, NEG_INF)
        m = jnp.max(s, axis=0, keepdims=True)
        p = jnp.exp(s - m)
        m_sc[c] = m
        l_sc[c] = jnp.sum(p, axis=0, keepdims=True)
        acc_sc[c] = _dot(vt_sc[c, qi], _bf(p))

    def past(j, carry):
        start = pl.multiple_of(j * tq, tq)
        for c in range(n_pairs):
            s = _dot(kcat_sc[c, pl.ds(start, tq), :], qs_sc[c])
            m_old = m_sc[c]
            m_new = jnp.maximum(m_old, jnp.max(s, axis=0, keepdims=True))
            alpha = jnp.exp(m_old - m_new)
            p = jnp.exp(s - m_new)
            m_sc[c] = m_new
            l_sc[c] = alpha * l_sc[c] + jnp.sum(p, axis=0, keepdims=True)
            acc_sc[c] = alpha * acc_sc[c] + _dot(vt_sc[c, j], _bf(p))
        return carry

    lax.fori_loop(0, qi, past, 0)

    dim_row = lax.broadcasted_iota(jnp.int32, (LANES, tq), 0)
    for c in range(n_pairs):
        o_t = acc_sc[c] / l_sc[c]
        o_ref[:, c * LANES:(c + 1) * LANES] = jnp.where(dim_row < MLA_V, o_t[:, 0:tq], o_t[:, tq:2 * tq]).T


def _mla_prompt(q_l, c_kv, kpe_slot, lw, batch, seq):
    tq = 256
    assert seq % tq == 0
    nq = seq // tq
    n_pairs = MLA_HEADS // 2
    const = lambda a: pl.BlockSpec(a.shape, lambda b, i: (0,) * a.ndim)
    return pl.pallas_call(
        _mla_prompt_kernel,
        grid=(batch, nq),
        in_specs=[pl.BlockSpec((tq, 768), lambda b, i: (b * nq + i, 0)),
                  pl.BlockSpec((seq, LANES), lambda b, i: (b, 0)),
                  pl.BlockSpec((seq, LANES), lambda b, i: (b, 0)),
                  const(lw["w_kv_up"]), const(lw["g_kn"]), const(lw["bd64"])],
        out_specs=pl.BlockSpec((tq, 384), lambda b, i: (b * nq + i, 0)),
        out_shape=jax.ShapeDtypeStruct((batch * seq, 384), F32),
        scratch_shapes=[pltpu.VMEM((n_pairs, seq, 256), BF16), pltpu.VMEM((n_pairs, nq, LANES, tq), BF16),
                        pltpu.VMEM((n_pairs, 256, 2 * tq), BF16), pltpu.VMEM((n_pairs, 1, 2 * tq), F32),
                        pltpu.VMEM((n_pairs, 1, 2 * tq), F32), pltpu.VMEM((n_pairs, LANES, 2 * tq), F32)],
        compiler_params=pltpu.CompilerParams(dimension_semantics=("arbitrary", "arbitrary"),
                                             vmem_limit_bytes=VMEM_LIMIT),
        name="mla_prompt",
    )(q_l, c_kv, kpe_slot, lw["w_kv_up"], lw["g_kn"], lw["bd64"])


def _retention_kernel(q_ref, k_ref, v_ref, gate_ref, st_ref, dec_ref, gpow_ref, wts_ref, gc_ref, bdm_ref,
                      gout_ref, bd64_ref, o_ref, sto_ref, st_sc):
    chunk = dec_ref.shape[-1]
    n_seqs, n_pairs = st_ref.shape[0], st_ref.shape[1]
    seq = q_ref.shape[0] // n_seqs
    lane = _lane((chunk, LANES))

    for i in range(n_seqs):
        for p in range(n_pairs):
            st_sc[p] = st_ref[i, p]

        def step(t, carry):
            r = pl.ds(pl.multiple_of(i * seq + t * chunk, chunk), chunk)
            for p in range(n_pairs):
                cols = slice(p * LANES, (p + 1) * LANES)
                q, k, v = q_ref[r, cols], k_ref[r, cols], v_ref[r, cols]
                kb, vb = _bf(k), _bf(v)
                halves = []
                for hh in range(2):
                    qh = _bf(jnp.where((lane // RET_DK) == hh, q, 0.0))
                    s = _dot_nt(qh, kb) * dec_ref[p, hh]
                    halves.append(_dot(_bf(s), vb))
                o = jnp.where(lane < RET_DV, halves[0], halves[1])
                o = o + _dot(_bf(q), _bf(st_sc[p])) * gpow_ref[p]
                u = _dot(_bf((k * wts_ref[p]).T), vb)
                st_sc[p] = st_sc[p] * gc_ref[p] + u * bdm_ref[...]
                o = o * lax.rsqrt(_group_mean_sq(o, bd64_ref[...]) + EPS) * gout_ref[...]
                o_ref[r, cols] = o * gate_ref[r, cols]
            return carry

        lax.fori_loop(0, seq // chunk, step, 0)
        for p in range(n_pairs):
            sto_ref[i, p] = st_sc[p]


def _retention_tables(chunk):
    lg = jnp.log(1.0 - 2.0 ** (-5.0 - jnp.arange(RET_HEADS, dtype=F32)))
    idx = jnp.arange(chunk, dtype=F32)
    diff = idx[:, None] - idx[None, :]
    decay = jnp.where(diff >= 0, jnp.exp(lg[:, None, None] * jnp.maximum(diff, 0.0)), 0.0)
    gpow = jnp.exp(lg[None, :] * (idx[:, None] + 1.0))
    wts = jnp.exp(lg[:, None] * (chunk - 1.0 - idx)[None, :]).T
    gc = jnp.exp(lg * chunk)
    n_pairs = RET_HEADS // 2
    lanes = lambda t: jnp.repeat(t.reshape(chunk, n_pairs, 2), RET_DV, axis=2).transpose(1, 0, 2)
    i = np.arange(LANES)
    bdm = jnp.asarray(((i[:, None] // RET_DK) == (i[None, :] // RET_DV)).astype(np.float32))
    gc_l = jnp.repeat(gc.reshape(n_pairs, 2), RET_DK, axis=1)
    gc_t = gc_l[:, :, None] * bdm[None]
    return decay.reshape(n_pairs, 2, chunk, chunk), lanes(gpow), lanes(wts), gc_t, bdm


def _retention(q_r, k_r, v_r, gate_r, state_bd, lw, batch, seq):
    chunk = min(RET_CHUNK, seq)
    assert seq % chunk == 0
    n_pairs = RET_HEADS // 2
    dec, gpow, wts, gc_t, bdm = _retention_tables(chunk)
    per_step = 8 if (seq == chunk and batch % 8 == 0) else 1
    act = pl.BlockSpec((per_step * seq, n_pairs * LANES), lambda b: (b, 0))
    st = pl.BlockSpec((per_step, n_pairs, LANES, LANES), lambda b: (b, 0, 0, 0))
    const = lambda a: pl.BlockSpec(a.shape, lambda b: (0,) * a.ndim)
    return pl.pallas_call(
        _retention_kernel,
        grid=(batch // per_step,),
        in_specs=[act, act, act, act, st, const(dec), const(gpow), const(wts), const(gc_t),
                  const(bdm), const(lw["g_ret_out"]), const(lw["bd64"])],
        out_specs=[act, st],
        out_shape=[jax.ShapeDtypeStruct((batch * seq, n_pairs * LANES), F32),
                   jax.ShapeDtypeStruct((batch, n_pairs, LANES, LANES), F32)],
        scratch_shapes=[pltpu.VMEM((n_pairs, LANES, LANES), F32)],
        compiler_params=pltpu.CompilerParams(dimension_semantics=("arbitrary",),
                                             vmem_limit_bytes=VMEM_LIMIT),
        name="retention",
    )(q_r, k_r, v_r, gate_r, state_bd, dec, gpow, wts, gc_t, bdm, lw["g_ret_out"], lw["bd64"])


def _state_to_block_diag(st):
    b = st.shape[0]
    s = st.reshape(b, RET_HEADS // 2, 2, RET_DK, RET_DV)
    eye = jnp.eye(2, dtype=st.dtype)
    return jnp.einsum("bpird,ij->bpirjd", s, eye).reshape(b, RET_HEADS // 2, 2 * RET_DK, 2 * RET_DV)


def _block_diag_to_state(bd):
    b = bd.shape[0]
    s = bd.reshape(b, RET_HEADS // 2, 2, RET_DK, 2, RET_DV)
    return jnp.stack([s[:, :, 0, :, 0, :], s[:, :, 1, :, 1, :]], axis=2).reshape(b, RET_HEADS, RET_DK, RET_DV)


def _out_mlp_kernel(x_ref, om_ref, or_ref, ol_ref, wom_ref, wor_ref, wol_ref, gmlp_ref, wup_ref, wdn_ref, y_ref):
    x1 = x_ref[...] + (_dot(_bf(om_ref[...]), wom_ref[...]) + _dot(_bf(or_ref[...]), wor_ref[...])
                       + _dot(_bf(ol_ref[...]), wol_ref[...]))
    h = _bf(_row_rms(x1) * gmlp_ref[...])
    d_ff = wup_ref.shape[1]
    step = min(1024, d_ff)
    acc = x1
    for c in range(d_ff // step):
        u = jnp.maximum(_dot(h, wup_ref[:, c * step:(c + 1) * step]), 0.0)
        acc = acc + _dot(_bf(u * u), wdn_ref[c * step:(c + 1) * step, :])
    y_ref[...] = acc


def _out_mlp(x2d, o_m, o_r, o_l, lw, tile):
    n, d = x2d.shape
    row = lambda w: pl.BlockSpec((tile, w), lambda i: (i, 0))
    const = lambda a: pl.BlockSpec(a.shape, lambda i: (0,) * a.ndim, pipeline_mode=pl.Buffered(1))
    consts = (lw["w_out_m"], lw["w_out_r"], lw["w_out_l"], lw["g_mlp"], lw["w_mlp_up"], lw["w_mlp_down"])
    return pl.pallas_call(
        _out_mlp_kernel,
        grid=(n // tile,),
        in_specs=[row(d), row(384), row(256), row(384)] + [const(a) for a in consts],
        out_specs=row(d),
        out_shape=jax.ShapeDtypeStruct((n, d), F32),
        compiler_params=pltpu.CompilerParams(dimension_semantics=("arbitrary",), vmem_limit_bytes=VMEM_LIMIT),
        name="out_mlp",
    )(x2d, o_m, o_r, o_l, *consts)


def _layer_weights(l, g_attn, w_in, g_moba_q, g_moba_k, g_ret_out, g_mla_qlat, w_mla_q_up, g_mla_qn, g_mla_qp,
                   g_mla_kvlat, w_mla_kv_up, g_mla_kn, g_mla_kp, w_out, g_mlp, w_mlp_up, w_mlp_down):
    ones64 = jnp.ones((64,), F32)
    gains = jnp.stack([
        jnp.tile(g_moba_q[l], 2), jnp.tile(g_moba_k[l], 2), g_mla_kvlat[l], jnp.tile(g_mla_qn[l], 2),
        jnp.concatenate([g_mla_qp[l], g_mla_qp[l], ones64]), jnp.concatenate([g_mla_kp[l], g_mla_kp[l], ones64]),
        jnp.ones((LANES,), F32), jnp.ones((LANES,), F32)])
    kv = w_mla_kv_up[l].reshape(MLA_KV_LORA, MLA_HEADS, MLA_NOPE + MLA_V)
    w_kn = kv[:, :, :MLA_NOPE].reshape(MLA_KV_LORA, MLA_HEADS * MLA_NOPE)
    w_v = kv[:, :, MLA_NOPE:].reshape(MLA_KV_LORA, MLA_HEADS * MLA_V)
    wo = w_out[l]
    perm = [(hk * MOBA_GROUP + g) for g in range(MOBA_GROUP) for hk in range(MOBA_KV_HEADS)]
    wo_m = wo[:MOBA_HEADS * HEAD_DIM].reshape(MOBA_HEADS, HEAD_DIM, -1)[jnp.asarray(perm)].reshape(MOBA_HEADS * HEAD_DIM, -1)
    r0 = MOBA_HEADS * HEAD_DIM
    r1 = r0 + RET_HEADS * RET_DV
    return {
        "g_attn": g_attn[l][None, :],
        "w_in": _bf(_take_cols(w_in[l], _in_proj_columns())),
        "w_q_up": _bf(_take_cols(w_mla_q_up[l], _q_up_columns())),
        "gains": gains,
        "g_qlat": g_mla_qlat[l][None, :],
        "bd64": _block_diag_avg(64),
        "bd32": _block_diag_avg(32),
        "w_kv_up": _bf(jnp.concatenate([w_kn, w_v], axis=1)),
        "w_kn_f32": w_kn,
        "w_v": _bf(w_v),
        "g_kn": jnp.tile(g_mla_kn[l], 2)[None, :],
        "g_ret_out": jnp.tile(g_ret_out[l], 2)[None, :],
        "w_out_m": _bf(wo_m), "w_out_r": _bf(wo[r0:r1]), "w_out_l": _bf(wo[r1:]),
        "g_mlp": g_mlp[l][None, :],
        "w_mlp_up": _bf(w_mlp_up[l]), "w_mlp_down": _bf(w_mlp_down[l]),
    }


def _rows_of_page(page):
    return lambda buf, slot, pg: buf.at[slot, pl.ds(pg * page, page), :]


def _whole_page(buf, slot, pg):
    return buf.at[slot, pg]


def _page_copies(pt_ref, layer, req, chunk, slot, pages_per_chunk, streams):
    copies = []
    for hbm, buf, sem, window in streams:
        for pg in range(pages_per_chunk):
            pid = pt_ref[req, chunk * pages_per_chunk + pg]
            copies.append(pltpu.make_async_copy(hbm.at[layer, pid], window(buf, slot, pg), sem.at[slot]))
    return copies


def _gather_step(pt_ref, layer, pages_per_chunk, streams):
    r, ch = pl.program_id(0), pl.program_id(1)
    n_req, n_ch = pl.num_programs(0), pl.num_programs(1)
    step = r * n_ch + ch
    slot = step % 2

    @pl.when(step == 0)
    def _():
        for c in _page_copies(pt_ref, layer, 0, 0, 0, pages_per_chunk, streams):
            c.start()

    @pl.when(step + 1 < n_req * n_ch)
    def _():
        wrap = ch + 1 == n_ch
        nr = jnp.where(wrap, r + 1, r)
        nc = jnp.where(wrap, 0, ch + 1)
        for c in _page_copies(pt_ref, layer, nr, nc, 1 - slot, pages_per_chunk, streams):
            c.start()

    for c in _page_copies(pt_ref, layer, r, ch, slot, pages_per_chunk, streams):
        c.wait()
    return slot


def _mla_decode_kernel(layer, pages_per_chunk, page, sub,
                       pt_ref, lat_hbm, kpe_hbm, qn_ref, qpe_ref, cnew_ref, penew_ref, wkt_ref, gkn_ref,
                       wv_ref, o_ref, cbuf, pbuf, sem_c, sem_p, lhs_sc, cb_sc, s_sc, m_sc, l_sc, acc_sc):
    ch, n_ch = pl.program_id(1), pl.num_programs(1)
    n_dec = cnew_ref.shape[0]
    rows = MLA_HEADS * n_dec
    n_y = MLA_HEADS * MLA_NOPE
    slot = _gather_step(pt_ref, layer, pages_per_chunk,
                        ((lat_hbm, cbuf, sem_c, _rows_of_page(page)), (kpe_hbm, pbuf, sem_p, _whole_page)))

    @pl.when(ch == 0)
    def _():
        lhs_sc[0:n_y, :] = wkt_ref[...]
        for h in range(MLA_HEADS):
            qn = qn_ref[0, h * n_dec:(h + 1) * n_dec, :] * gkn_ref[...]
            lhs_sc[n_y + h * n_dec:n_y + (h + 1) * n_dec, :] = _dot(
                _bf(qn), _bf(wkt_ref[h * MLA_NOPE:(h + 1) * MLA_NOPE, :]))
        m_sc[...] = jnp.full(m_sc.shape, NEG_INF, F32)
        l_sc[...] = jnp.zeros(l_sc.shape, F32)
        acc_sc[...] = jnp.zeros(acc_sc.shape, F32)

    lhs = _bf(lhs_sc[...])
    qpe = _bf(qpe_ref[0])
    scale = MLA_QK ** -0.5

    def scores(cb, s_pe):
        r = _dot_nt(lhs, cb)
        per_head = []
        for h in range(MLA_HEADS):
            y = r[h * MLA_NOPE:(h + 1) * MLA_NOPE, :]
            rinv = lax.rsqrt(jnp.mean(y * y, axis=0, keepdims=True) + EPS)
            per_head.append(r[n_y + h * n_dec:n_y + (h + 1) * n_dec, :] * rinv)
        return (jnp.concatenate(per_head, axis=0) + s_pe) * scale

    def update(s, cb):
        m_old = m_sc[...]
        m_new = jnp.maximum(m_old, jnp.max(s, axis=-1, keepdims=True))
        alpha = jnp.exp(m_old - m_new)
        p = jnp.exp(s - m_new)
        m_sc[...] = m_new
        l_sc[...] = alpha * l_sc[...] + jnp.sum(p, axis=-1, keepdims=True)
        acc_sc[...] = alpha * acc_sc[...] + _dot(_bf(p), cb)

    pages_per_sub = sub // page
    for t in range((pages_per_chunk * page) // sub):
        rws = slice(t * sub, (t + 1) * sub)
        cb = _bf(cbuf[slot, rws, :])
        cb_sc[rws, :] = cb
        s_pe = jnp.concatenate([_dot(qpe, _bf(pbuf[slot, t * pages_per_sub + i])) for i in range(pages_per_sub)],
                               axis=1)
        s_sc[:, rws] = scores(cb, s_pe)
    update(s_sc[...], cb_sc[...])

    @pl.when(ch == n_ch - 1)
    def _():
        cb = _bf(cnew_ref[...])
        s = scores(cb, _dot_nt(qpe, _bf(penew_ref[0])))
        t_row = lax.broadcasted_iota(jnp.int32, s.shape, 0) % n_dec
        update(jnp.where(_lane(s.shape) <= t_row, s, NEG_INF), cb)
        o_lat = _bf(acc_sc[...] / l_sc[...])
        lane = _lane((n_dec, LANES))
        for c in range(MLA_HEADS // 2):
            wv = wv_ref[:, c * LANES:(c + 1) * LANES]
            o0 = _dot(o_lat[(2 * c) * n_dec:(2 * c + 1) * n_dec, :], wv)
            o1 = _dot(o_lat[(2 * c + 1) * n_dec:(2 * c + 2) * n_dec, :], wv)
            o_ref[:, c * LANES:(c + 1) * LANES] = jnp.where(lane < MLA_V, o0, o1)


def _decode_chunk_keys(past_len, page):
    keys = min(2048, max(page, past_len // 2))
    assert past_len % keys == 0 and keys % page == 0, (past_len, keys, page)
    return keys


def _mla_decode(layer, page_table, lat_cache, kpe_cache, q_l, c_new, kpe_slot, lw, n_req, n_dec):
    page = lat_cache.shape[2]
    past_len = page_table.shape[1] * page
    keys = _decode_chunk_keys(past_len, page)
    sub = min(512, keys)
    rows = MLA_HEADS * n_dec
    arr = q_l.reshape(n_req, n_dec, MLA_HEADS // 2, 256)
    heads_first = lambda a, w: a.reshape(n_req, n_dec, MLA_HEADS // 2, 2, w).transpose(0, 2, 3, 1, 4).reshape(n_req, rows, w)
    qn_rows = heads_first(arr[..., :LANES], MLA_NOPE)
    qpe_rows = heads_first(arr[..., LANES:LANES + 2 * MLA_ROPE], MLA_ROPE)
    pe_new = kpe_slot[:, :MLA_ROPE].reshape(n_req, n_dec, MLA_ROPE)
    assert sub % page == 0 and page == LANES
    w_kn_t = lw["w_kn_f32"].T
    g_kn = lw["g_kn"][:, :MLA_NOPE]
    const = lambda a: pl.BlockSpec(a.shape, lambda r, c, pt: (0,) * a.ndim)
    kern = functools.partial(_mla_decode_kernel, layer, keys // page, page, sub)
    return pl.pallas_call(
        kern,
        grid_spec=pltpu.PrefetchScalarGridSpec(
            num_scalar_prefetch=1,
            grid=(n_req, past_len // keys),
            in_specs=[pl.BlockSpec(memory_space=pl.ANY), pl.BlockSpec(memory_space=pl.ANY),
                      pl.BlockSpec((1, rows, MLA_NOPE), lambda r, c, pt: (r, 0, 0)),
                      pl.BlockSpec((1, rows, MLA_ROPE), lambda r, c, pt: (r, 0, 0)),
                      pl.BlockSpec((n_dec, LANES), lambda r, c, pt: (r, 0)),
                      pl.BlockSpec((1, n_dec, MLA_ROPE), lambda r, c, pt: (r, 0, 0)),
                      const(w_kn_t), const(g_kn), const(lw["w_v"])],
            out_specs=pl.BlockSpec((n_dec, MLA_HEADS * MLA_V), lambda r, c, pt: (r, 0)),
            scratch_shapes=[pltpu.VMEM((2, keys, LANES), F32), pltpu.VMEM((2, keys // page, MLA_ROPE, page), F32),
                            pltpu.SemaphoreType.DMA((2,)), pltpu.SemaphoreType.DMA((2,)),
                            pltpu.VMEM((MLA_HEADS * MLA_NOPE + rows, LANES), F32),
                            pltpu.VMEM((keys, LANES), BF16), pltpu.VMEM((rows, keys), F32),
                            pltpu.VMEM((rows, 1), F32), pltpu.VMEM((rows, 1), F32), pltpu.VMEM((rows, LANES), F32)]),
        out_shape=jax.ShapeDtypeStruct((n_req * n_dec, MLA_HEADS * MLA_V), F32),
        compiler_params=pltpu.CompilerParams(dimension_semantics=("arbitrary", "arbitrary"),
                                             vmem_limit_bytes=VMEM_LIMIT),
        name="mla_decode",
    )(page_table, lat_cache, kpe_cache.transpose(0, 1, 3, 2), qn_rows, qpe_rows, c_new, pe_new, w_kn_t, g_kn,
      lw["w_v"])


def _moba_decode_kernel(layer, pages_per_chunk, page,
                        pt_ref, k_hbm, v_hbm, q_ref, knew_ref, vnew_ref, o_ref,
                        kbuf, vbuf, sem_k, sem_v, vall_sc, s_sc, kmean_sc, bmax_sc):
    ch, n_ch = pl.program_id(1), pl.num_programs(1)
    n_dec = knew_ref.shape[0]
    keys = pages_per_chunk * page
    blocks_per_chunk = keys // MOBA_BLOCK
    slot = _gather_step(pt_ref, layer, pages_per_chunk,
                        ((k_hbm, kbuf, sem_k, _whole_page), (v_hbm, vbuf, sem_v, _whole_page)))
    q = q_ref[0]
    qb = _bf(q * (HEAD_DIM ** -0.5))
    n_blk = kmean_sc.shape[1]
    pages_per_block = MOBA_BLOCK // page

    @pl.when(ch == 0)
    def _():
        kmean_sc[...] = jnp.zeros(kmean_sc.shape, F32)
        bmax_sc[...] = jnp.full(bmax_sc.shape, NEG_INF, F32)

    blk_col = _lane((page, n_blk))
    k_pages = [kbuf[slot, pg].reshape(LANES, page) for pg in range(pages_per_chunk)]
    s = _dot(qb, _bf(jnp.concatenate(k_pages, axis=1)))
    s_sc[ch] = s
    bmax = bmax_sc[...]
    for b in range(blocks_per_chunk):
        best = s[:, b * MOBA_BLOCK:b * MOBA_BLOCK + page]
        for i in range(1, pages_per_block):
            best = jnp.maximum(best, s[:, b * MOBA_BLOCK + i * page:b * MOBA_BLOCK + (i + 1) * page])
        bmax = jnp.where(_lane(bmax.shape) == ch * blocks_per_chunk + b, jnp.max(best, axis=-1, keepdims=True), bmax)
    bmax_sc[...] = bmax
    for pg in range(pages_per_chunk):
        vall_sc[ch, :, pg * page:(pg + 1) * page] = _bf(vbuf[slot, pg].reshape(LANES, page))
    kmean = kmean_sc[...]
    for b in range(blocks_per_chunk):
        ksum = k_pages[b * pages_per_block]
        for i in range(1, pages_per_block):
            ksum = ksum + k_pages[b * pages_per_block + i]
        hi, lo = _split_bf16(ksum)
        pick = _bf(jnp.where(blk_col == ch * blocks_per_chunk + b, 1.0 / MOBA_BLOCK, 0.0))
        kmean = kmean + (_dot(hi, pick) + _dot(lo, pick))
    kmean_sc[...] = kmean

    @pl.when(ch == n_ch - 1)
    def _():
        sel_f = _topk_mask(_dot_f32(q, kmean_sc[...]), n_blk, MOBA_TOPK)
        sel = _bf(sel_f)
        s_own = _dot_nt(qb, _bf(knew_ref[...]))
        t_row = lax.broadcasted_iota(jnp.int32, s_own.shape, 0) % n_dec
        s_own = jnp.where(_lane(s_own.shape) <= t_row, s_own, NEG_INF)
        rows = sel.shape[0]
        blk_of_key = (lax.broadcasted_iota(jnp.int32, (n_blk, keys), 1) // MOBA_BLOCK
                      - lax.broadcasted_iota(jnp.int32, (n_blk, keys), 0))

        def chunk_mask(c):
            return _dot(sel, _bf(jnp.where(blk_of_key + c * blocks_per_chunk == 0, 1.0, 0.0))) > 0.5

        def fold(x, op):
            out = x[:, 0:page]
            for i in range(1, keys // page):
                out = op(out, x[:, i * page:(i + 1) * page])
            return out

        m_sel = jnp.max(jnp.where(sel_f > 0.5, bmax_sc[...], NEG_INF), axis=-1, keepdims=True)
        m = jnp.maximum(m_sel, jnp.max(s_own, axis=-1, keepdims=True))

        def chunk_pv(c, carry):
            lsum, acc = carry
            p = jnp.where(chunk_mask(c), jnp.exp(s_sc[c] - m), 0.0)
            return lsum + fold(p, jnp.add), acc + _dot_nt(_bf(p), vall_sc[c])

        p_own = jnp.exp(s_own - m)
        lsum, acc = lax.fori_loop(0, s_sc.shape[0], chunk_pv,
                                  (jnp.zeros((rows, page), F32), _dot(_bf(p_own), _bf(vnew_ref[...]))),
                                  unroll=min(4, s_sc.shape[0]))
        l = jnp.sum(lsum, axis=-1, keepdims=True) + jnp.sum(p_own, axis=-1, keepdims=True)
        o_ref[0] = acc / l


def _moba_decode(layer, page_table, k_cache, v_cache, q_m, k_new, v_new, n_req, n_dec):
    page = k_cache.shape[2]
    past_len = page_table.shape[1] * page
    assert past_len % MOBA_BLOCK == 0 and MOBA_BLOCK % page == 0
    keys = _decode_chunk_keys(past_len, page)
    assert keys % MOBA_BLOCK == 0
    n_blk = past_len // MOBA_BLOCK
    rows = MOBA_HEADS * n_dec
    qg = q_m.reshape(n_req, n_dec, MOBA_GROUP, LANES).transpose(0, 2, 1, 3)
    half = (np.arange(LANES) // HEAD_DIM)[None, :] == np.arange(MOBA_KV_HEADS)[:, None]
    q_rows = (qg[:, None] * jnp.asarray(half.astype(np.float32))[None, :, None, None, :]).reshape(n_req, rows, LANES)
    k_t = k_cache.transpose(0, 1, 3, 4, 2)
    v_t = v_cache.transpose(0, 1, 3, 4, 2)
    page_buf = pltpu.VMEM((2, keys // page, MOBA_KV_HEADS, HEAD_DIM, page), F32)
    kern = functools.partial(_moba_decode_kernel, layer, keys // page, page)
    o_rows = pl.pallas_call(
        kern,
        grid_spec=pltpu.PrefetchScalarGridSpec(
            num_scalar_prefetch=1,
            grid=(n_req, past_len // keys),
            in_specs=[pl.BlockSpec(memory_space=pl.ANY), pl.BlockSpec(memory_space=pl.ANY),
                      pl.BlockSpec((1, rows, LANES), lambda r, c, pt: (r, 0, 0)),
                      pl.BlockSpec((n_dec, LANES), lambda r, c, pt: (r, 0)),
                      pl.BlockSpec((n_dec, LANES), lambda r, c, pt: (r, 0))],
            out_specs=pl.BlockSpec((1, rows, LANES), lambda r, c, pt: (r, 0, 0)),
            scratch_shapes=[page_buf, page_buf,
                            pltpu.SemaphoreType.DMA((2,)), pltpu.SemaphoreType.DMA((2,)),
                            pltpu.VMEM((past_len // keys, LANES, keys), BF16),
                            pltpu.VMEM((past_len // keys, rows, keys), F32),
                            pltpu.VMEM((LANES, n_blk), F32), pltpu.VMEM((rows, n_blk), F32)]),
        out_shape=jax.ShapeDtypeStruct((n_req, rows, LANES), F32),
        compiler_params=pltpu.CompilerParams(dimension_semantics=("arbitrary", "arbitrary"),
                                             vmem_limit_bytes=VMEM_LIMIT),
        name="moba_decode",
    )(page_table, k_t, v_t, q_rows, k_new, v_new)
    o = o_rows.reshape(n_req, MOBA_KV_HEADS, MOBA_GROUP, n_dec, LANES)
    pairs = jnp.concatenate([o[:, 0, :, :, :HEAD_DIM], o[:, 1, :, :, HEAD_DIM:]], axis=-1)
    return pairs.transpose(0, 2, 1, 3).reshape(n_req * n_dec, MOBA_GROUP * LANES)


def _layer_sample(layer, x2d, lw, caches, state, page_table, n_req, n_dec):
    ck, cv, clat, ckpe = caches
    past_len = page_table.shape[1] * ck.shape[2]
    n = n_req * n_dec
    tile = _tile_rows(n, 256)
    assert tile % n_dec == 0
    pos = jnp.tile(past_len + jnp.arange(n_dec), tile // n_dec)
    q_m, k_m, v_m, q_r, k_r, v_r, gate_r, q_l, c_kv, kpe = _project(x2d, pos, 1, lw, tile)
    o_m = _moba_decode(layer, page_table, ck, cv, q_m, k_m, v_m, n_req, n_dec)
    o_r, st = _retention(q_r, k_r, v_r, gate_r, _state_to_block_diag(state.astype(F32)), lw, n_req, n_dec)
    o_l = _mla_decode(layer, page_table, clat, ckpe, q_l, c_kv, kpe, lw, n_req, n_dec)
    y = _out_mlp(x2d, o_m, o_r, o_l, lw, _tile_rows(n, 512))
    new = (k_m.reshape(n_req, n_dec, MOBA_KV_HEADS, HEAD_DIM), v_m.reshape(n_req, n_dec, MOBA_KV_HEADS, HEAD_DIM),
           c_kv.reshape(n_req, n_dec, MLA_KV_LORA), kpe[:, :MLA_ROPE].reshape(n_req, n_dec, MLA_ROPE),
           _block_diag_to_state(st).astype(state.dtype))
    return y, new


def _layer_prompt(x2d, lw, batch, seq):
    tile = _tile_rows(seq, 256)
    q_m, k_m, v_m, q_r, k_r, v_r, gate_r, q_l, c_kv, kpe = _project(
        x2d, jnp.arange(seq), seq // tile, lw, tile)
    o_m = _moba_prompt(q_m, k_m, v_m, batch, seq)
    st0 = jnp.zeros((batch, RET_HEADS // 2, LANES, LANES), F32)
    o_r, st = _retention(q_r, k_r, v_r, gate_r, st0, lw, batch, seq)
    o_l = _mla_prompt(q_l, c_kv, kpe, lw, batch, seq)
    y = _out_mlp(x2d, o_m, o_r, o_l, lw, _tile_rows(x2d.shape[0], 512))
    new = (k_m.reshape(batch, seq, MOBA_KV_HEADS, HEAD_DIM), v_m.reshape(batch, seq, MOBA_KV_HEADS, HEAD_DIM),
           c_kv.reshape(batch, seq, MLA_KV_LORA), kpe[:, :MLA_ROPE].reshape(batch, seq, MLA_ROPE),
           _block_diag_to_state(st))
    return y, new


def kernel(x_prompt, x_sample, cache_moba_k, cache_moba_v, cache_mla_latent, cache_mla_kpe, state_ret, page_table,
           g_attn, w_in, g_moba_q, g_moba_k, g_ret_out, g_mla_qlat, w_mla_q_up, g_mla_qn, g_mla_qp, g_mla_kvlat,
           w_mla_kv_up, g_mla_kn, g_mla_kp, w_out, g_mlp, w_mlp_up, w_mlp_down):
    batch, seq, d_model = x_prompt.shape
    n_req, n_dec, _ = x_sample.shape
    depth = w_in.shape[0]
    params = (g_attn, w_in, g_moba_q, g_moba_k, g_ret_out, g_mla_qlat, w_mla_q_up, g_mla_qn, g_mla_qp,
              g_mla_kvlat, w_mla_kv_up, g_mla_kn, g_mla_kp, w_out, g_mlp, w_mlp_up, w_mlp_down)
    yp = x_prompt.reshape(batch * seq, d_model)
    ys = x_sample.reshape(n_req * n_dec, d_model)
    caches = (cache_moba_k, cache_moba_v, cache_mla_latent, cache_mla_kpe)
    new_p, new_s = [], []
    for l in range(depth):
        lw = _layer_weights(l, *params)
        yp, st_p = _layer_prompt(yp, lw, batch, seq)
        ys, st_s = _layer_sample(l, ys, lw, caches, state_ret[l], page_table, n_req, n_dec)
        new_p.append(st_p)
        new_s.append(st_s)
    stacked_p = [jnp.stack(a) for a in zip(*new_p)]
    stacked_s = [jnp.stack(a) for a in zip(*new_s)]
    return (yp.reshape(batch, seq, d_model), ys.reshape(n_req, n_dec, d_model), *stacked_p, *stacked_s)
```

```python
import functools

import numpy as np
import jax
import jax.numpy as jnp
from jax import lax
from jax.experimental import pallas as pl
from jax.experimental.pallas import tpu as pltpu

HEAD_DIM = 64
MOBA_HEADS = 6
MOBA_KV_HEADS = 2
MOBA_GROUP = MOBA_HEADS // MOBA_KV_HEADS
MOBA_BLOCK = 256
MOBA_TOPK = 3
RET_HEADS = 4
RET_DK = 64
RET_DV = 64
RET_CHUNK = 128
MLA_HEADS = 6
MLA_Q_LORA = 256
MLA_KV_LORA = 128
MLA_NOPE = 64
MLA_ROPE = 32
MLA_V = 64
MLA_QK = MLA_NOPE + MLA_ROPE
ROPE_THETA = 10000.0
EPS = 1e-6

LANES = 128
VMEM_LIMIT = 56 * 1024 * 1024
F32 = jnp.float32
BF16 = jnp.bfloat16
NEG_INF = float("-inf")

N_IN_CHUNKS = 17


def _bf(x):
    return x.astype(BF16)


def _dot(a, b):
    return jnp.dot(a, b, preferred_element_type=F32)


def _dot_nt(a, b):
    return lax.dot_general(a, b, (((1,), (1,)), ((), ())), preferred_element_type=F32)


def _split_bf16(x):
    hi = _bf(x)
    lo = _bf(x - hi.astype(F32))
    return hi, lo


def _dot_nt_f32(a, b):
    ah, al = _split_bf16(a)
    bh, bl = _split_bf16(b)
    return _dot_nt(ah, bh) + (_dot_nt(ah, bl) + _dot_nt(al, bh))


def _dot_f32(a, b):
    ah, al = _split_bf16(a)
    bh, bl = _split_bf16(b)
    return _dot(ah, bh) + (_dot(ah, bl) + _dot(al, bh))


def _group_mean_sq(x, bd):
    hi, lo = _split_bf16(x * x)
    return _dot(hi, bd) + _dot(lo, bd)


def _lane(shape):
    return lax.broadcasted_iota(jnp.int32, shape, len(shape) - 1)


def _rotate_half(x, half):
    n = x.shape[-1]
    up = pltpu.roll(x, n - half, x.ndim - 1)
    down = pltpu.roll(x, half, x.ndim - 1)
    return jnp.where(_lane(x.shape) % (2 * half) < half, up, down)


def _rope(x, cos, sin_signed, half):
    return x * cos + _rotate_half(x, half) * sin_signed


def _row_rms(x):
    return x * lax.rsqrt(jnp.mean(x * x, axis=-1, keepdims=True) + EPS)


def _proj_kernel(x_ref, gattn_ref, win_ref, wq_ref, gains_ref, gqlat_ref, bd64_ref, bd32_ref,
                 cos64_ref, sin64_ref, cos32_ref, sin32_ref,
                 qm_ref, km_ref, vm_ref, qr_ref, kr_ref, vr_ref, gr_ref, ql_ref, ckv_ref, kpe_ref):
    x = x_ref[...]
    h = _row_rms(x) * gattn_ref[...]
    z = _dot(_bf(h), win_ref[...])
    bd64 = bd64_ref[...]
    bd32 = bd32_ref[...]
    cos64, sin64 = cos64_ref[...], sin64_ref[...]
    cos32, sin32 = cos32_ref[...], sin32_ref[...]
    g_mq, g_mk = gains_ref[0:1, :], gains_ref[1:2, :]
    g_kvlat, g_qn = gains_ref[2:3, :], gains_ref[3:4, :]
    g_qp, g_kp = gains_ref[4:5, :], gains_ref[5:6, :]

    def chunk(i):
        return z[:, i * LANES:(i + 1) * LANES]

    def norm_group(v, bd, g):
        return v * lax.rsqrt(_group_mean_sq(v, bd) + EPS) * g

    for i in range(3):
        qm_ref[:, i * LANES:(i + 1) * LANES] = _rope(norm_group(chunk(i), bd64, g_mq), cos64, sin64, 32)
    km_ref[...] = _rope(norm_group(chunk(3), bd64, g_mk), cos64, sin64, 32)
    vm_ref[...] = chunk(4)
    for i in range(2):
        qr_ref[:, i * LANES:(i + 1) * LANES] = _rope(chunk(5 + i), cos64, sin64, 32)
        kr_ref[:, i * LANES:(i + 1) * LANES] = _rope(chunk(7 + i), cos64, sin64, 32) * (RET_DK ** -0.5)
        vr_ref[:, i * LANES:(i + 1) * LANES] = chunk(9 + i)
        g = chunk(11 + i)
        gr_ref[:, i * LANES:(i + 1) * LANES] = g * jax.nn.sigmoid(g)
    zq = z[:, 13 * LANES:15 * LANES]
    ql = _dot(_bf(_row_rms(zq) * gqlat_ref[...]), wq_ref[...])
    for c in range(3):
        nope = ql[:, c * 256:c * 256 + LANES]
        pe = ql[:, c * 256 + LANES:(c + 1) * 256]
        ql_ref[:, c * 256:c * 256 + LANES] = norm_group(nope, bd64, g_qn)
        ql_ref[:, c * 256 + LANES:(c + 1) * 256] = _rope(norm_group(pe, bd32, g_qp), cos32, sin32, 16)
    ckv_ref[...] = _row_rms(chunk(15)) * g_kvlat
    kpe_ref[...] = _rope(norm_group(chunk(16), bd32, g_kp), cos32, sin32, 16)


def _in_proj_columns():
    cols = []
    for g in range(MOBA_GROUP):
        for hk in range(MOBA_KV_HEADS):
            h = hk * MOBA_GROUP + g
            cols += list(range(h * HEAD_DIM, (h + 1) * HEAD_DIM))
    cols += list(range(384, 2048))
    cols += list(range(2048, 2080)) * 2 + [-1] * 64
    return np.asarray(cols, np.int32)


def _q_up_columns():
    cols = []
    for c in range(MLA_HEADS // 2):
        for h in (2 * c, 2 * c + 1):
            cols += list(range(h * MLA_QK, h * MLA_QK + MLA_NOPE))
        for h in (2 * c, 2 * c + 1):
            cols += list(range(h * MLA_QK + MLA_NOPE, (h + 1) * MLA_QK))
        cols += [-1] * 64
    return np.asarray(cols, np.int32)


def _take_cols(w, cols):
    out = jnp.take(w, jnp.asarray(np.maximum(cols, 0)), axis=1)
    return out * jnp.asarray((cols >= 0).astype(np.float32))[None, :]


def _block_diag_avg(group):
    i = np.arange(LANES)
    return jnp.asarray(((i[:, None] // group) == (i[None, :] // group)).astype(np.float32) / group, BF16)


def _rope_tables(pos, dim):
    half = dim // 2
    inv = ROPE_THETA ** (-jnp.arange(half, dtype=F32) / half)
    ang = pos.astype(F32)[:, None] * inv[None, :]
    cos, sin = jnp.cos(ang), jnp.sin(ang)
    reps = LANES // dim
    return jnp.tile(jnp.concatenate([cos, cos], -1), (1, reps)), jnp.tile(jnp.concatenate([-sin, sin], -1), (1, reps))


def _tile_rows(n_rows, cap):
    t = min(cap, n_rows)
    assert n_rows % t == 0 and t % 8 == 0, (n_rows, t)
    return t


def _project(x2d, pos, n_pos_tiles, lw, tile):
    n, d = x2d.shape
    c64, s64 = _rope_tables(pos, 64)
    c32, s32 = _rope_tables(pos, 32)
    grid = (n // tile,)
    row = lambda w: pl.BlockSpec((tile, w), lambda i: (i, 0))
    const = lambda a: pl.BlockSpec(a.shape, lambda i: (0,) * a.ndim)
    tab = pl.BlockSpec((tile, LANES), lambda i: (i % n_pos_tiles, 0))
    widths = (384, 128, 128, 256, 256, 256, 256, 768, 128, 128)
    consts = (lw["g_attn"], lw["w_in"], lw["w_q_up"], lw["gains"], lw["g_qlat"], lw["bd64"], lw["bd32"])
    return pl.pallas_call(
        _proj_kernel,
        grid=grid,
        in_specs=[row(d)] + [const(a) for a in consts] + [tab] * 4,
        out_specs=[row(w) for w in widths],
        out_shape=[jax.ShapeDtypeStruct((n, w), F32) for w in widths],
        compiler_params=pltpu.CompilerParams(dimension_semantics=("arbitrary",), vmem_limit_bytes=VMEM_LIMIT),
        name="in_proj",
    )(x2d, *consts, c64, s64, c32, s32)


def _topk_mask(gate, n_valid, k):
    col = _lane(gate.shape)
    g = jnp.where(col < n_valid, gate, NEG_INF)
    sel = jnp.zeros(gate.shape, F32)
    big = jnp.int32(gate.shape[-1])
    for _ in range(k):
        m = jnp.max(g, axis=-1, keepdims=True)
        idx = jnp.min(jnp.where(g == m, col, big), axis=-1, keepdims=True)
        pick = (col == idx) & (m > NEG_INF)
        sel = jnp.where(pick, 1.0, sel)
        g = jnp.where(pick, NEG_INF, g)
    return sel


def _topk_mask_t(gate, n_valid, k):
    row = lax.broadcasted_iota(jnp.int32, gate.shape, 0)
    g = jnp.where(row < n_valid, gate, NEG_INF)
    sel = jnp.zeros(gate.shape, F32)
    big = jnp.int32(gate.shape[0])
    for _ in range(k):
        m = jnp.max(g, axis=0, keepdims=True)
        idx = jnp.min(jnp.where(g == m, row, big), axis=0, keepdims=True)
        pick = (row == idx) & (m > NEG_INF)
        sel = jnp.where(pick, 1.0, sel)
        g = jnp.where(pick, NEG_INF, g)
    return sel


def _moba_prompt_kernel(q_ref, k_ref, v_ref, o_ref, kb_sc, vt_sc, kmean_sc, qf_sc, qs_sc, sel_sc, m_sc, l_sc, acc_sc):
    tq = q_ref.shape[0]
    n_blk = kmean_sc.shape[0]
    qi = pl.program_id(1)

    @pl.when(qi == 0)
    def _():
        kb_sc[...] = _bf(k_ref[...])
        for j in range(n_blk):
            blk = slice(j * MOBA_BLOCK, (j + 1) * MOBA_BLOCK)
            vt_sc[j] = _bf(v_ref[blk, :].T)
            kmean_sc[j:j + 1, :] = jnp.mean(k_ref[blk, :], axis=0, keepdims=True)

    dim_row = lax.broadcasted_iota(jnp.int32, (LANES, tq), 0)
    for g in range(MOBA_GROUP):
        chunk_t = q_ref[:, g * LANES:(g + 1) * LANES].T
        for hk in range(MOBA_KV_HEADS):
            s = g * MOBA_KV_HEADS + hk
            qf_sc[:, s * tq:(s + 1) * tq] = jnp.where((dim_row // HEAD_DIM) == hk, chunk_t, 0.0)
    q_t = qf_sc[...]
    sel_sc[...] = _topk_mask_t(_dot_f32(kmean_sc[...], q_t), qi, MOBA_TOPK)
    qs_sc[...] = _bf(q_t * (HEAD_DIM ** -0.5))

    own = pl.multiple_of(qi * MOBA_BLOCK, MOBA_BLOCK)
    s = _dot(kb_sc[pl.ds(own, MOBA_BLOCK), :], qs_sc[...])
    kpos = lax.broadcasted_iota(jnp.int32, s.shape, 0)
    s = jnp.where(kpos <= _lane(s.shape) % tq, s, NEG_INF)
    m = jnp.max(s, axis=0, keepdims=True)
    p = jnp.exp(s - m)
    m_sc[...] = m
    l_sc[...] = jnp.sum(p, axis=0, keepdims=True)
    acc_sc[...] = _dot(vt_sc[qi], _bf(p))

    def past(j, carry):
        start = pl.multiple_of(j * MOBA_BLOCK, MOBA_BLOCK)
        chosen = sel_sc[pl.ds(j, 1), :] > 0.5
        s = jnp.where(chosen, _dot(kb_sc[pl.ds(start, MOBA_BLOCK), :], qs_sc[...]), NEG_INF)
        m_old = m_sc[...]
        m_new = jnp.maximum(m_old, jnp.max(s, axis=0, keepdims=True))
        alpha = jnp.exp(m_old - m_new)
        p = jnp.exp(s - m_new)
        m_sc[...] = m_new
        l_sc[...] = alpha * l_sc[...] + jnp.sum(p, axis=0, keepdims=True)
        acc_sc[...] = alpha * acc_sc[...] + _dot(vt_sc[j], _bf(p))
        return carry

    lax.fori_loop(0, qi, past, 0)

    o_t = acc_sc[...] / l_sc[...]
    for g in range(MOBA_GROUP):
        c0, c1 = (2 * g) * tq, (2 * g + 1) * tq
        pair_t = jnp.where(dim_row < HEAD_DIM, o_t[:, c0:c0 + tq], o_t[:, c1:c1 + tq])
        o_ref[:, g * LANES:(g + 1) * LANES] = pair_t.T


def _moba_prompt(q_m, k_m, v_m, batch, seq):
    assert seq % MOBA_BLOCK == 0
    tq = MOBA_BLOCK
    n_blk = seq // MOBA_BLOCK
    nq = seq // tq
    rows = MOBA_HEADS * tq
    return pl.pallas_call(
        _moba_prompt_kernel,
        grid=(batch, nq),
        in_specs=[pl.BlockSpec((tq, 384), lambda b, i: (b * nq + i, 0)),
                  pl.BlockSpec((seq, LANES), lambda b, i: (b, 0)),
                  pl.BlockSpec((seq, LANES), lambda b, i: (b, 0))],
        out_specs=pl.BlockSpec((tq, 384), lambda b, i: (b * nq + i, 0)),
        out_shape=jax.ShapeDtypeStruct((batch * seq, 384), F32),
        scratch_shapes=[pltpu.VMEM((seq, LANES), BF16), pltpu.VMEM((n_blk, LANES, MOBA_BLOCK), BF16),
                        pltpu.VMEM((n_blk, LANES), F32), pltpu.VMEM((LANES, rows), F32),
                        pltpu.VMEM((LANES, rows), BF16), pltpu.VMEM((n_blk, rows), F32),
                        pltpu.VMEM((1, rows), F32), pltpu.VMEM((1, rows), F32), pltpu.VMEM((LANES, rows), F32)],
        compiler_params=pltpu.CompilerParams(dimension_semantics=("arbitrary", "arbitrary"),
                                             vmem_limit_bytes=VMEM_LIMIT),
        name="moba_prompt",
    )(q_m, k_m, v_m)


def _mla_prompt_kernel(q_ref, ckv_ref, kpe_ref, wkv_ref, gkn_ref, bd64_ref, o_ref,
                       kcat_sc, vt_sc, qs_sc, m_sc, l_sc, acc_sc):
    tq = q_ref.shape[0]
    seq = ckv_ref.shape[0]
    n_pairs = MLA_HEADS // 2
    qi = pl.program_id(1)

    @pl.when(qi == 0)
    def _():
        def expand(t, carry):
            r = pl.ds(pl.multiple_of(t * tq, tq), tq)
            kv = _dot(_bf(ckv_ref[r, :]), wkv_ref[...])
            pe = _bf(kpe_ref[r, :])
            for c in range(n_pairs):
                kn = kv[:, c * LANES:(c + 1) * LANES]
                kn = kn * lax.rsqrt(_group_mean_sq(kn, bd64_ref[...]) + EPS) * gkn_ref[...]
                kcat_sc[c, r, 0:LANES] = _bf(kn)
                kcat_sc[c, r, LANES:2 * LANES] = pe
                vt_sc[c, t] = _bf(kv[:, (n_pairs + c) * LANES:(n_pairs + c + 1) * LANES].T)
            return carry
        lax.fori_loop(0, seq // tq, expand, 0)

    depth = lax.broadcasted_iota(jnp.int32, (256, tq), 0)
    for c in range(n_pairs):
        chunk_t = (q_ref[:, c * 256:(c + 1) * 256] * (MLA_QK ** -0.5)).T
        for hh in range(2):
            nope = (depth < LANES) & ((depth // MLA_NOPE) == hh)
            pe = (depth >= LANES) & (depth < LANES + 2 * MLA_ROPE) & (((depth - LANES) // MLA_ROPE) == hh)
            qs_sc[c, :, hh * tq:(hh + 1) * tq] = _bf(jnp.where(nope | pe, chunk_t, 0.0))

    own = pl.multiple_of(qi * tq, tq)
    for c in range(n_pairs):
        s = _dot(kcat_sc[c, pl.ds(own, tq), :], qs_sc[c])
        kpos = lax.broadcasted_iota(jnp.int32, s.shape, 0)
        s = jnp.where(kpos <= _lane(s.shape) % tq, s, NEG_INF)
        m = jnp.max(s, axis=0, keepdims=True)
        p = jnp.exp(s - m)
        m_sc[c] = m
        l_sc[c] = jnp.sum(p, axis=0, keepdims=True)
        acc_sc[c] = _dot(vt_sc[c, qi], _bf(p))

    def past(j, carry):
        start = pl.multiple_of(j * tq, tq)
        for c in range(n_pairs):
            s = _dot(kcat_sc[c, pl.ds(start, tq), :], qs_sc[c])
            m_old = m_sc[c]
            m_new = jnp.maximum(m_old, jnp.max(s, axis=0, keepdims=True))
            alpha = jnp.exp(m_old - m_new)
            p = jnp.exp(s - m_new)
            m_sc[c] = m_new
            l_sc[c] = alpha * l_sc[c] + jnp.sum(p, axis=0, keepdims=True)
            acc_sc[c] = alpha * acc_sc[c] + _dot(vt_sc[c, j], _bf(p))
        return carry

    lax.fori_loop(0, qi, past, 0)

    dim_row = lax.broadcasted_iota(jnp.int32, (LANES, tq), 0)
    for c in range(n_pairs):
        o_t = acc_sc[c] / l_sc[c]
        o_ref[:, c * LANES:(c + 1) * LANES] = jnp.where(dim_row < MLA_V, o_t[:, 0:tq], o_t[:, tq:2 * tq]).T


def _mla_prompt(q_l, c_kv, kpe_slot, lw, batch, seq):
    tq = 256
    assert seq % tq == 0
    nq = seq // tq
    n_pairs = MLA_HEADS // 2
    const = lambda a: pl.BlockSpec(a.shape, lambda b, i: (0,) * a.ndim)
    return pl.pallas_call(
        _mla_prompt_kernel,
        grid=(batch, nq),
        in_specs=[pl.BlockSpec((tq, 768), lambda b, i: (b * nq + i, 0)),
                  pl.BlockSpec((seq, LANES), lambda b, i: (b, 0)),
                  pl.BlockSpec((seq, LANES), lambda b, i: (b, 0)),
                  const(lw["w_kv_up"]), const(lw["g_kn"]), const(lw["bd64"])],
        out_specs=pl.BlockSpec((tq, 384), lambda b, i: (b * nq + i, 0)),
        out_shape=jax.ShapeDtypeStruct((batch * seq, 384), F32),
        scratch_shapes=[pltpu.VMEM((n_pairs, seq, 256), BF16), pltpu.VMEM((n_pairs, nq, LANES, tq), BF16),
                        pltpu.VMEM((n_pairs, 256, 2 * tq), BF16), pltpu.VMEM((n_pairs, 1, 2 * tq), F32),
                        pltpu.VMEM((n_pairs, 1, 2 * tq), F32), pltpu.VMEM((n_pairs, LANES, 2 * tq), F32)],
        compiler_params=pltpu.CompilerParams(dimension_semantics=("arbitrary", "arbitrary"),
                                             vmem_limit_bytes=VMEM_LIMIT),
        name="mla_prompt",
    )(q_l, c_kv, kpe_slot, lw["w_kv_up"], lw["g_kn"], lw["bd64"])


def _retention_kernel(q_ref, k_ref, v_ref, gate_ref, st_ref, dec_ref, gpow_ref, wts_ref, gc_ref, bdm_ref,
                      gout_ref, bd64_ref, o_ref, sto_ref, st_sc):
    chunk = dec_ref.shape[-1]
    n_seqs, n_pairs = st_ref.shape[0], st_ref.shape[1]
    seq = q_ref.shape[0] // n_seqs
    lane = _lane((chunk, LANES))

    for i in range(n_seqs):
        for p in range(n_pairs):
            st_sc[p] = st_ref[i, p]

        def step(t, carry):
            r = pl.ds(pl.multiple_of(i * seq + t * chunk, chunk), chunk)
            for p in range(n_pairs):
                cols = slice(p * LANES, (p + 1) * LANES)
                q, k, v = q_ref[r, cols], k_ref[r, cols], v_ref[r, cols]
                kb, vb = _bf(k), _bf(v)
                halves = []
                for hh in range(2):
                    qh = _bf(jnp.where((lane // RET_DK) == hh, q, 0.0))
                    s = _dot_nt(qh, kb) * dec_ref[p, hh]
                    halves.append(_dot(_bf(s), vb))
                o = jnp.where(lane < RET_DV, halves[0], halves[1])
                o = o + _dot(_bf(q), _bf(st_sc[p])) * gpow_ref[p]
                u = _dot(_bf((k * wts_ref[p]).T), vb)
                st_sc[p] = st_sc[p] * gc_ref[p] + u * bdm_ref[...]
                o = o * lax.rsqrt(_group_mean_sq(o, bd64_ref[...]) + EPS) * gout_ref[...]
                o_ref[r, cols] = o * gate_ref[r, cols]
            return carry

        lax.fori_loop(0, seq // chunk, step, 0)
        for p in range(n_pairs):
            sto_ref[i, p] = st_sc[p]


def _retention_tables(chunk):
    lg = jnp.log(1.0 - 2.0 ** (-5.0 - jnp.arange(RET_HEADS, dtype=F32)))
    idx = jnp.arange(chunk, dtype=F32)
    diff = idx[:, None] - idx[None, :]
    decay = jnp.where(diff >= 0, jnp.exp(lg[:, None, None] * jnp.maximum(diff, 0.0)), 0.0)
    gpow = jnp.exp(lg[None, :] * (idx[:, None] + 1.0))
    wts = jnp.exp(lg[:, None] * (chunk - 1.0 - idx)[None, :]).T
    gc = jnp.exp(lg * chunk)
    n_pairs = RET_HEADS // 2
    lanes = lambda t: jnp.repeat(t.reshape(chunk, n_pairs, 2), RET_DV, axis=2).transpose(1, 0, 2)
    i = np.arange(LANES)
    bdm = jnp.asarray(((i[:, None] // RET_DK) == (i[None, :] // RET_DV)).astype(np.float32))
    gc_l = jnp.repeat(gc.reshape(n_pairs, 2), RET_DK, axis=1)
    gc_t = gc_l[:, :, None] * bdm[None]
    return decay.reshape(n_pairs, 2, chunk, chunk), lanes(gpow), lanes(wts), gc_t, bdm


def _retention(q_r, k_r, v_r, gate_r, state_bd, lw, batch, seq):
    chunk = min(RET_CHUNK, seq)
    assert seq % chunk == 0
    n_pairs = RET_HEADS // 2
    dec, gpow, wts, gc_t, bdm = _retention_tables(chunk)
    per_step = 8 if (seq == chunk and batch % 8 == 0) else 1
    act = pl.BlockSpec((per_step * seq, n_pairs * LANES), lambda b: (b, 0))
    st = pl.BlockSpec((per_step, n_pairs, LANES, LANES), lambda b: (b, 0, 0, 0))
    const = lambda a: pl.BlockSpec(a.shape, lambda b: (0,) * a.ndim)
    return pl.pallas_call(
        _retention_kernel,
        grid=(batch // per_step,),
        in_specs=[act, act, act, act, st, const(dec), const(gpow), const(wts), const(gc_t),
                  const(bdm), const(lw["g_ret_out"]), const(lw["bd64"])],
        out_specs=[act, st],
        out_shape=[jax.ShapeDtypeStruct((batch * seq, n_pairs * LANES), F32),
                   jax.ShapeDtypeStruct((batch, n_pairs, LANES, LANES), F32)],
        scratch_shapes=[pltpu.VMEM((n_pairs, LANES, LANES), F32)],
        compiler_params=pltpu.CompilerParams(dimension_semantics=("arbitrary",),
                                             vmem_limit_bytes=VMEM_LIMIT),
        name="retention",
    )(q_r, k_r, v_r, gate_r, state_bd, dec, gpow, wts, gc_t, bdm, lw["g_ret_out"], lw["bd64"])


def _state_to_block_diag(st):
    b = st.shape[0]
    s = st.reshape(b, RET_HEADS // 2, 2, RET_DK, RET_DV)
    eye = jnp.eye(2, dtype=st.dtype)
    return jnp.einsum("bpird,ij->bpirjd", s, eye).reshape(b, RET_HEADS // 2, 2 * RET_DK, 2 * RET_DV)


def _block_diag_to_state(bd):
    b = bd.shape[0]
    s = bd.reshape(b, RET_HEADS // 2, 2, RET_DK, 2, RET_DV)
    return jnp.stack([s[:, :, 0, :, 0, :], s[:, :, 1, :, 1, :]], axis=2).reshape(b, RET_HEADS, RET_DK, RET_DV)


def _out_mlp_kernel(x_ref, om_ref, or_ref, ol_ref, wom_ref, wor_ref, wol_ref, gmlp_ref, wup_ref, wdn_ref, y_ref):
    x1 = x_ref[...] + (_dot(_bf(om_ref[...]), wom_ref[...]) + _dot(_bf(or_ref[...]), wor_ref[...])
                       + _dot(_bf(ol_ref[...]), wol_ref[...]))
    h = _bf(_row_rms(x1) * gmlp_ref[...])
    d_ff = wup_ref.shape[1]
    step = min(1024, d_ff)
    acc = x1
    for c in range(d_ff // step):
        u = jnp.maximum(_dot(h, wup_ref[:, c * step:(c + 1) * step]), 0.0)
        acc = acc + _dot(_bf(u * u), wdn_ref[c * step:(c + 1) * step, :])
    y_ref[...] = acc


def _out_mlp(x2d, o_m, o_r, o_l, lw, tile):
    n, d = x2d.shape
    row = lambda w: pl.BlockSpec((tile, w), lambda i: (i, 0))
    const = lambda a: pl.BlockSpec(a.shape, lambda i: (0,) * a.ndim, pipeline_mode=pl.Buffered(1))
    consts = (lw["w_out_m"], lw["w_out_r"], lw["w_out_l"], lw["g_mlp"], lw["w_mlp_up"], lw["w_mlp_down"])
    return pl.pallas_call(
        _out_mlp_kernel,
        grid=(n // tile,),
        in_specs=[row(d), row(384), row(256), row(384)] + [const(a) for a in consts],
        out_specs=row(d),
        out_shape=jax.ShapeDtypeStruct((n, d), F32),
        compiler_params=pltpu.CompilerParams(dimension_semantics=("arbitrary",), vmem_limit_bytes=VMEM_LIMIT),
        name="out_mlp",
    )(x2d, o_m, o_r, o_l, *consts)


def _layer_weights(l, g_attn, w_in, g_moba_q, g_moba_k, g_ret_out, g_mla_qlat, w_mla_q_up, g_mla_qn, g_mla_qp,
                   g_mla_kvlat, w_mla_kv_up, g_mla_kn, g_mla_kp, w_out, g_mlp, w_mlp_up, w_mlp_down):
    ones64 = jnp.ones((64,), F32)
    gains = jnp.stack([
        jnp.tile(g_moba_q[l], 2), jnp.tile(g_moba_k[l], 2), g_mla_kvlat[l], jnp.tile(g_mla_qn[l], 2),
        jnp.concatenate([g_mla_qp[l], g_mla_qp[l], ones64]), jnp.concatenate([g_mla_kp[l], g_mla_kp[l], ones64]),
        jnp.ones((LANES,), F32), jnp.ones((LANES,), F32)])
    kv = w_mla_kv_up[l].reshape(MLA_KV_LORA, MLA_HEADS, MLA_NOPE + MLA_V)
    w_kn = kv[:, :, :MLA_NOPE].reshape(MLA_KV_LORA, MLA_HEADS * MLA_NOPE)
    w_v = kv[:, :, MLA_NOPE:].reshape(MLA_KV_LORA, MLA_HEADS * MLA_V)
    wo = w_out[l]
    perm = [(hk * MOBA_GROUP + g) for g in range(MOBA_GROUP) for hk in range(MOBA_KV_HEADS)]
    wo_m = wo[:MOBA_HEADS * HEAD_DIM].reshape(MOBA_HEADS, HEAD_DIM, -1)[jnp.asarray(perm)].reshape(MOBA_HEADS * HEAD_DIM, -1)
    r0 = MOBA_HEADS * HEAD_DIM
    r1 = r0 + RET_HEADS * RET_DV
    return {
        "g_attn": g_attn[l][None, :],
        "w_in": _bf(_take_cols(w_in[l], _in_proj_columns())),
        "w_q_up": _bf(_take_cols(w_mla_q_up[l], _q_up_columns())),
        "gains": gains,
        "g_qlat": g_mla_qlat[l][None, :],
        "bd64": _block_diag_avg(64),
        "bd32": _block_diag_avg(32),
        "w_kv_up": _bf(jnp.concatenate([w_kn, w_v], axis=1)),
        "w_kn_f32": w_kn,
        "w_v": _bf(w_v),
        "g_kn": jnp.tile(g_mla_kn[l], 2)[None, :],
        "g_ret_out": jnp.tile(g_ret_out[l], 2)[None, :],
        "w_out_m": _bf(wo_m), "w_out_r": _bf(wo[r0:r1]), "w_out_l": _bf(wo[r1:]),
        "g_mlp": g_mlp[l][None, :],
        "w_mlp_up": _bf(w_mlp_up[l]), "w_mlp_down": _bf(w_mlp_down[l]),
    }


def _rows_of_page(page):
    return lambda buf, slot, pg: buf.at[slot, pl.ds(pg * page, page), :]


def _whole_page(buf, slot, pg):
    return buf.at[slot, pg]


def _page_copies(pt_ref, layer, req, chunk, slot, pages_per_chunk, streams):
    copies = []
    for hbm, buf, sem, window in streams:
        for pg in range(pages_per_chunk):
            pid = pt_ref[req, chunk * pages_per_chunk + pg]
            copies.append(pltpu.make_async_copy(hbm.at[layer, pid], window(buf, slot, pg), sem.at[slot]))
    return copies


def _gather_first(pt_ref, layer, pages_per_chunk, streams):
    @pl.when(pl.program_id(0) == 0)
    def _():
        for c in _page_copies(pt_ref, layer, 0, 0, 0, pages_per_chunk, streams):
            c.start()


def _gather_chunk(pt_ref, layer, ch, n_ch, pages_per_chunk, streams):
    r, n_req = pl.program_id(0), pl.num_programs(0)
    slot = (r * n_ch + ch) % 2
    wrap = ch + 1 == n_ch
    nr = jnp.where(wrap, jnp.minimum(r + 1, n_req - 1), r)
    nc = jnp.where(wrap, 0, ch + 1)
    for c in _page_copies(pt_ref, layer, nr, nc, 1 - slot, pages_per_chunk, streams):
        c.start()
    for c in _page_copies(pt_ref, layer, r, ch, slot, pages_per_chunk, streams):
        c.wait()
    return slot


def _gather_drain(pt_ref, layer, n_ch, pages_per_chunk, streams):
    r, n_req = pl.program_id(0), pl.num_programs(0)

    @pl.when(r == n_req - 1)
    def _():
        slot = (n_req * n_ch) % 2
        for c in _page_copies(pt_ref, layer, r, 0, slot, pages_per_chunk, streams):
            c.wait()


def _mla_decode_kernel(layer, n_ch, pages_per_chunk, page, sub,
                       pt_ref, lat_hbm, kpe_hbm, qn_ref, qpe_ref, cnew_ref, penew_ref, wkt_ref, gkn_ref,
                       wv_ref, o_ref, cbuf, pbuf, sem_c, sem_p, lhs_sc, cb_sc, s_sc, m_sc, l_sc, acc_sc):
    n_dec = cnew_ref.shape[0]
    rows = MLA_HEADS * n_dec
    n_y = MLA_HEADS * MLA_NOPE
    streams = ((lat_hbm, cbuf, sem_c, _rows_of_page(page)), (kpe_hbm, pbuf, sem_p, _whole_page))
    _gather_first(pt_ref, layer, pages_per_chunk, streams)

    lhs_sc[0:n_y, :] = wkt_ref[...]
    for h in range(MLA_HEADS):
        qn = qn_ref[0, h * n_dec:(h + 1) * n_dec, :] * gkn_ref[...]
        lhs_sc[n_y + h * n_dec:n_y + (h + 1) * n_dec, :] = _dot(
            _bf(qn), _bf(wkt_ref[h * MLA_NOPE:(h + 1) * MLA_NOPE, :]))
    m_sc[...] = jnp.full(m_sc.shape, NEG_INF, F32)
    l_sc[...] = jnp.zeros(l_sc.shape, F32)
    acc_sc[...] = jnp.zeros(acc_sc.shape, F32)

    lhs = _bf(lhs_sc[...])
    qpe = _bf(qpe_ref[0])
    scale = MLA_QK ** -0.5

    def scores(cb, s_pe):
        r = _dot_nt(lhs, cb)
        per_head = []
        for h in range(MLA_HEADS):
            y = r[h * MLA_NOPE:(h + 1) * MLA_NOPE, :]
            rinv = lax.rsqrt(jnp.mean(y * y, axis=0, keepdims=True) + EPS)
            per_head.append(r[n_y + h * n_dec:n_y + (h + 1) * n_dec, :] * rinv)
        return (jnp.concatenate(per_head, axis=0) + s_pe) * scale

    def update(s, cb):
        m_old = m_sc[...]
        m_new = jnp.maximum(m_old, jnp.max(s, axis=-1, keepdims=True))
        alpha = jnp.exp(m_old - m_new)
        p = jnp.exp(s - m_new)
        m_sc[...] = m_new
        l_sc[...] = alpha * l_sc[...] + jnp.sum(p, axis=-1, keepdims=True)
        acc_sc[...] = alpha * acc_sc[...] + _dot(_bf(p), cb)

    def chunk(ch, carry):
        slot = _gather_chunk(pt_ref, layer, ch, n_ch, pages_per_chunk, streams)
        s_pe = _dot(qpe, _bf(jnp.concatenate([pbuf[slot, pg] for pg in range(pages_per_chunk)], axis=1)))
        for t in range((pages_per_chunk * page) // sub):
            rws = slice(t * sub, (t + 1) * sub)
            cb = _bf(cbuf[slot, rws, :])
            cb_sc[rws, :] = cb
            s_sc[:, rws] = scores(cb, s_pe[:, rws])
        update(s_sc[...], cb_sc[...])
        return carry

    lax.fori_loop(0, n_ch, chunk, 0)

    cb = _bf(cnew_ref[...])
    s = scores(cb, _dot_nt(qpe, _bf(penew_ref[0])))
    t_row = lax.broadcasted_iota(jnp.int32, s.shape, 0) % n_dec
    update(jnp.where(_lane(s.shape) <= t_row, s, NEG_INF), cb)
    o_lat = _bf(acc_sc[...] / l_sc[...])
    lane = _lane((n_dec, LANES))
    for c in range(MLA_HEADS // 2):
        wv = wv_ref[:, c * LANES:(c + 1) * LANES]
        o0 = _dot(o_lat[(2 * c) * n_dec:(2 * c + 1) * n_dec, :], wv)
        o1 = _dot(o_lat[(2 * c + 1) * n_dec:(2 * c + 2) * n_dec, :], wv)
        o_ref[:, c * LANES:(c + 1) * LANES] = jnp.where(lane < MLA_V, o0, o1)
    _gather_drain(pt_ref, layer, n_ch, pages_per_chunk, streams)


def _decode_chunk_keys(past_len, page):
    keys = min(2048, max(page, past_len // 2))
    assert past_len % keys == 0 and keys % page == 0, (past_len, keys, page)
    return keys


def _mla_decode(layer, page_table, lat_cache, kpe_cache, q_l, c_new, kpe_slot, lw, n_req, n_dec):
    page = lat_cache.shape[2]
    past_len = page_table.shape[1] * page
    keys = _decode_chunk_keys(past_len, page)
    sub = min(256, keys)
    rows = MLA_HEADS * n_dec
    arr = q_l.reshape(n_req, n_dec, MLA_HEADS // 2, 256)
    heads_first = lambda a, w: a.reshape(n_req, n_dec, MLA_HEADS // 2, 2, w).transpose(0, 2, 3, 1, 4).reshape(n_req, rows, w)
    qn_rows = heads_first(arr[..., :LANES], MLA_NOPE)
    qpe_rows = heads_first(arr[..., LANES:LANES + 2 * MLA_ROPE], MLA_ROPE)
    pe_new = kpe_slot[:, :MLA_ROPE].reshape(n_req, n_dec, MLA_ROPE)
    assert sub % page == 0 and page == LANES
    w_kn_t = lw["w_kn_f32"].T
    g_kn = lw["g_kn"][:, :MLA_NOPE]
    const = lambda a: pl.BlockSpec(a.shape, lambda r, pt: (0,) * a.ndim)
    kern = functools.partial(_mla_decode_kernel, layer, past_len // keys, keys // page, page, sub)
    return pl.pallas_call(
        kern,
        grid_spec=pltpu.PrefetchScalarGridSpec(
            num_scalar_prefetch=1,
            grid=(n_req,),
            in_specs=[pl.BlockSpec(memory_space=pl.ANY), pl.BlockSpec(memory_space=pl.ANY),
                      pl.BlockSpec((1, rows, MLA_NOPE), lambda r, pt: (r, 0, 0)),
                      pl.BlockSpec((1, rows, MLA_ROPE), lambda r, pt: (r, 0, 0)),
                      pl.BlockSpec((n_dec, LANES), lambda r, pt: (r, 0)),
                      pl.BlockSpec((1, n_dec, MLA_ROPE), lambda r, pt: (r, 0, 0)),
                      const(w_kn_t), const(g_kn), const(lw["w_v"])],
            out_specs=pl.BlockSpec((n_dec, MLA_HEADS * MLA_V), lambda r, pt: (r, 0)),
            scratch_shapes=[pltpu.VMEM((2, keys, LANES), F32), pltpu.VMEM((2, keys // page, MLA_ROPE, page), F32),
                            pltpu.SemaphoreType.DMA((2,)), pltpu.SemaphoreType.DMA((2,)),
                            pltpu.VMEM((MLA_HEADS * MLA_NOPE + rows, LANES), F32),
                            pltpu.VMEM((keys, LANES), BF16), pltpu.VMEM((rows, keys), F32),
                            pltpu.VMEM((rows, 1), F32), pltpu.VMEM((rows, 1), F32), pltpu.VMEM((rows, LANES), F32)]),
        out_shape=jax.ShapeDtypeStruct((n_req * n_dec, MLA_HEADS * MLA_V), F32),
        compiler_params=pltpu.CompilerParams(dimension_semantics=("arbitrary",),
                                             vmem_limit_bytes=VMEM_LIMIT),
        name="mla_decode",
    )(page_table, lat_cache, kpe_cache.transpose(0, 1, 3, 2), qn_rows, qpe_rows, c_new, pe_new, w_kn_t, g_kn,
      lw["w_v"])


def _moba_decode_kernel(layer, n_ch, pages_per_chunk, page,
                        pt_ref, k_hbm, v_hbm, q_ref, knew_ref, vnew_ref, o_ref,
                        kbuf, vbuf, sem_k, sem_v, vall_sc, s_sc):
    n_dec = knew_ref.shape[0]
    keys = pages_per_chunk * page
    blocks_per_chunk = keys // MOBA_BLOCK
    n_blk = n_ch * blocks_per_chunk
    pages_per_block = MOBA_BLOCK // page
    streams = ((k_hbm, kbuf, sem_k, _whole_page), (v_hbm, vbuf, sem_v, _whole_page))
    _gather_first(pt_ref, layer, pages_per_chunk, streams)
    q = q_ref[0]
    qb = _bf(q * (HEAD_DIM ** -0.5))
    rows = q.shape[0]
    blk_col = _lane((page, n_blk))

    def chunk(ch, carry):
        kmean, bmax = carry
        slot = _gather_chunk(pt_ref, layer, ch, n_ch, pages_per_chunk, streams)
        k_pages = [kbuf[slot, pg].reshape(LANES, page) for pg in range(pages_per_chunk)]
        s = _dot(qb, _bf(jnp.concatenate(k_pages, axis=1)))
        s_sc[ch] = s
        for b in range(blocks_per_chunk):
            best = s[:, b * MOBA_BLOCK:b * MOBA_BLOCK + page]
            for i in range(1, pages_per_block):
                best = jnp.maximum(best, s[:, b * MOBA_BLOCK + i * page:b * MOBA_BLOCK + (i + 1) * page])
            bmax = jnp.where(_lane(bmax.shape) == ch * blocks_per_chunk + b,
                             jnp.max(best, axis=-1, keepdims=True), bmax)
        for pg in range(pages_per_chunk):
            vall_sc[ch, :, pg * page:(pg + 1) * page] = _bf(vbuf[slot, pg].reshape(LANES, page))
        for b in range(blocks_per_chunk):
            ksum = k_pages[b * pages_per_block]
            for i in range(1, pages_per_block):
                ksum = ksum + k_pages[b * pages_per_block + i]
            hi, lo = _split_bf16(ksum)
            pick = _bf(jnp.where(blk_col == ch * blocks_per_chunk + b, 1.0 / MOBA_BLOCK, 0.0))
            kmean = kmean + (_dot(hi, pick) + _dot(lo, pick))
        return kmean, bmax

    kmean, bmax = lax.fori_loop(0, n_ch, chunk,
                                (jnp.zeros((LANES, n_blk), F32), jnp.full((rows, n_blk), NEG_INF, F32)))

    sel_f = _topk_mask(_dot_f32(q, kmean), n_blk, MOBA_TOPK)
    sel = _bf(sel_f)
    s_own = _dot_nt(qb, _bf(knew_ref[...]))
    t_row = lax.broadcasted_iota(jnp.int32, s_own.shape, 0) % n_dec
    s_own = jnp.where(_lane(s_own.shape) <= t_row, s_own, NEG_INF)
    blk_of_key = (lax.broadcasted_iota(jnp.int32, (n_blk, keys), 1) // MOBA_BLOCK
                  - lax.broadcasted_iota(jnp.int32, (n_blk, keys), 0))

    def chunk_mask(c):
        return _dot(sel, _bf(jnp.where(blk_of_key + c * blocks_per_chunk == 0, 1.0, 0.0))) > 0.5

    def fold(x, op):
        out = x[:, 0:page]
        for i in range(1, keys // page):
            out = op(out, x[:, i * page:(i + 1) * page])
        return out

    m_sel = jnp.max(jnp.where(sel_f > 0.5, bmax, NEG_INF), axis=-1, keepdims=True)
    m = jnp.maximum(m_sel, jnp.max(s_own, axis=-1, keepdims=True))

    def chunk_pv(c, carry):
        lsum, acc = carry
        p = jnp.where(chunk_mask(c), jnp.exp(s_sc[c] - m), 0.0)
        return lsum + fold(p, jnp.add), acc + _dot_nt(_bf(p), vall_sc[c])

    p_own = jnp.exp(s_own - m)
    lsum, acc = lax.fori_loop(0, n_ch, chunk_pv,
                              (jnp.zeros((rows, page), F32), _dot(_bf(p_own), _bf(vnew_ref[...]))),
                              unroll=min(4, n_ch))
    l = jnp.sum(lsum, axis=-1, keepdims=True) + jnp.sum(p_own, axis=-1, keepdims=True)
    o_ref[0] = acc / l
    _gather_drain(pt_ref, layer, n_ch, pages_per_chunk, streams)


def _moba_decode(layer, page_table, k_cache, v_cache, q_m, k_new, v_new, n_req, n_dec):
    page = k_cache.shape[2]
    past_len = page_table.shape[1] * page
    assert past_len % MOBA_BLOCK == 0 and MOBA_BLOCK % page == 0
    keys = _decode_chunk_keys(past_len, page)
    assert keys % MOBA_BLOCK == 0
    n_blk = past_len // MOBA_BLOCK
    rows = MOBA_HEADS * n_dec
    qg = q_m.reshape(n_req, n_dec, MOBA_GROUP, LANES).transpose(0, 2, 1, 3)
    half = (np.arange(LANES) // HEAD_DIM)[None, :] == np.arange(MOBA_KV_HEADS)[:, None]
    q_rows = (qg[:, None] * jnp.asarray(half.astype(np.float32))[None, :, None, None, :]).reshape(n_req, rows, LANES)
    k_t = k_cache.transpose(0, 1, 3, 4, 2)
    v_t = v_cache.transpose(0, 1, 3, 4, 2)
    page_buf = pltpu.VMEM((2, keys // page, MOBA_KV_HEADS, HEAD_DIM, page), F32)
    kern = functools.partial(_moba_decode_kernel, layer, past_len // keys, keys // page, page)
    o_rows = pl.pallas_call(
        kern,
        grid_spec=pltpu.PrefetchScalarGridSpec(
            num_scalar_prefetch=1,
            grid=(n_req,),
            in_specs=[pl.BlockSpec(memory_space=pl.ANY), pl.BlockSpec(memory_space=pl.ANY),
                      pl.BlockSpec((1, rows, LANES), lambda r, pt: (r, 0, 0)),
                      pl.BlockSpec((n_dec, LANES), lambda r, pt: (r, 0)),
                      pl.BlockSpec((n_dec, LANES), lambda r, pt: (r, 0))],
            out_specs=pl.BlockSpec((1, rows, LANES), lambda r, pt: (r, 0, 0)),
            scratch_shapes=[page_buf, page_buf,
                            pltpu.SemaphoreType.DMA((2,)), pltpu.SemaphoreType.DMA((2,)),
                            pltpu.VMEM((past_len // keys, LANES, keys), BF16),
                            pltpu.VMEM((past_len // keys, rows, keys), F32)]),
        out_shape=jax.ShapeDtypeStruct((n_req, rows, LANES), F32),
        compiler_params=pltpu.CompilerParams(dimension_semantics=("arbitrary",),
                                             vmem_limit_bytes=VMEM_LIMIT),
        name="moba_decode",
    )(page_table, k_t, v_t, q_rows, k_new, v_new)
    o = o_rows.reshape(n_req, MOBA_KV_HEADS, MOBA_GROUP, n_dec, LANES)
    pairs = jnp.concatenate([o[:, 0, :, :, :HEAD_DIM], o[:, 1, :, :, HEAD_DIM:]], axis=-1)
    return pairs.transpose(0, 2, 1, 3).reshape(n_req * n_dec, MOBA_GROUP * LANES)


def _layer_sample(layer, x2d, lw, caches, state, page_table, n_req, n_dec):
    ck, cv, clat, ckpe = caches
    past_len = page_table.shape[1] * ck.shape[2]
    n = n_req * n_dec
    tile = _tile_rows(n, 256)
    assert tile % n_dec == 0
    pos = jnp.tile(past_len + jnp.arange(n_dec), tile // n_dec)
    q_m, k_m, v_m, q_r, k_r, v_r, gate_r, q_l, c_kv, kpe = _project(x2d, pos, 1, lw, tile)
    o_m = _moba_decode(layer, page_table, ck, cv, q_m, k_m, v_m, n_req, n_dec)
    o_r, st = _retention(q_r, k_r, v_r, gate_r, _state_to_block_diag(state.astype(F32)), lw, n_req, n_dec)
    o_l = _mla_decode(layer, page_table, clat, ckpe, q_l, c_kv, kpe, lw, n_req, n_dec)
    y = _out_mlp(x2d, o_m, o_r, o_l, lw, _tile_rows(n, 512))
    new = (k_m.reshape(n_req, n_dec, MOBA_KV_HEADS, HEAD_DIM), v_m.reshape(n_req, n_dec, MOBA_KV_HEADS, HEAD_DIM),
           c_kv.reshape(n_req, n_dec, MLA_KV_LORA), kpe[:, :MLA_ROPE].reshape(n_req, n_dec, MLA_ROPE),
           _block_diag_to_state(st).astype(state.dtype))
    return y, new


def _layer_prompt(x2d, lw, batch, seq):
    tile = _tile_rows(seq, 256)
    q_m, k_m, v_m, q_r, k_r, v_r, gate_r, q_l, c_kv, kpe = _project(
        x2d, jnp.arange(seq), seq // tile, lw, tile)
    o_m = _moba_prompt(q_m, k_m, v_m, batch, seq)
    st0 = jnp.zeros((batch, RET_HEADS // 2, LANES, LANES), F32)
    o_r, st = _retention(q_r, k_r, v_r, gate_r, st0, lw, batch, seq)
    o_l = _mla_prompt(q_l, c_kv, kpe, lw, batch, seq)
    y = _out_mlp(x2d, o_m, o_r, o_l, lw, _tile_rows(x2d.shape[0], 512))
    new = (k_m.reshape(batch, seq, MOBA_KV_HEADS, HEAD_DIM), v_m.reshape(batch, seq, MOBA_KV_HEADS, HEAD_DIM),
           c_kv.reshape(batch, seq, MLA_KV_LORA), kpe[:, :MLA_ROPE].reshape(batch, seq, MLA_ROPE),
           _block_diag_to_state(st))
    return y, new


def kernel(x_prompt, x_sample, cache_moba_k, cache_moba_v, cache_mla_latent, cache_mla_kpe, state_ret, page_table,
           g_attn, w_in, g_moba_q, g_moba_k, g_ret_out, g_mla_qlat, w_mla_q_up, g_mla_qn, g_mla_qp, g_mla_kvlat,
           w_mla_kv_up, g_mla_kn, g_mla_kp, w_out, g_mlp, w_mlp_up, w_mlp_down):
    batch, seq, d_model = x_prompt.shape
    n_req, n_dec, _ = x_sample.shape
    depth = w_in.shape[0]
    params = (g_attn, w_in, g_moba_q, g_moba_k, g_ret_out, g_mla_qlat, w_mla_q_up, g_mla_qn, g_mla_qp,
              g_mla_kvlat, w_mla_kv_up, g_mla_kn, g_mla_kp, w_out, g_mlp, w_mlp_up, w_mlp_down)
    yp = x_prompt.reshape(batch * seq, d_model)
    ys = x_sample.reshape(n_req * n_dec, d_model)
    caches = (cache_moba_k, cache_moba_v, cache_mla_latent, cache_mla_kpe)
    new_p, new_s = [], []
    for l in range(depth):
        lw = _layer_weights(l, *params)
        yp, st_p = _layer_prompt(yp, lw, batch, seq)
        ys, st_s = _layer_sample(l, ys, lw, caches, state_ret[l], page_table, n_req, n_dec)
        new_p.append(st_p)
        new_s.append(st_s)
    stacked_p = [jnp.stack(a) for a in zip(*new_p)]
    stacked_s = [jnp.stack(a) for a in zip(*new_s)]
    return (yp.reshape(batch, seq, d_model), ys.reshape(n_req, n_dec, d_model), *stacked_p, *stacked_s)
```

```python
import functools

import numpy as np
import jax
import jax.numpy as jnp
from jax import lax
from jax.experimental import pallas as pl
from jax.experimental.pallas import tpu as pltpu

HEAD_DIM = 64
MOBA_HEADS = 6
MOBA_KV_HEADS = 2
MOBA_GROUP = MOBA_HEADS // MOBA_KV_HEADS
MOBA_BLOCK = 256
MOBA_TOPK = 3
RET_HEADS = 4
RET_DK = 64
RET_DV = 64
RET_CHUNK = 128
MLA_HEADS = 6
MLA_Q_LORA = 256
MLA_KV_LORA = 128
MLA_NOPE = 64
MLA_ROPE = 32
MLA_V = 64
MLA_QK = MLA_NOPE + MLA_ROPE
ROPE_THETA = 10000.0
EPS = 1e-6

LANES = 128
VMEM_LIMIT = 56 * 1024 * 1024
F32 = jnp.float32
BF16 = jnp.bfloat16
NEG_INF = float("-inf")

N_IN_CHUNKS = 17


def _bf(x):
    return x.astype(BF16)


def _dot(a, b):
    return jnp.dot(a, b, preferred_element_type=F32)


def _dot_nt(a, b):
    return lax.dot_general(a, b, (((1,), (1,)), ((), ())), preferred_element_type=F32)


def _split_bf16(x):
    hi = _bf(x)
    lo = _bf(x - hi.astype(F32))
    return hi, lo


def _dot_nt_f32(a, b):
    ah, al = _split_bf16(a)
    bh, bl = _split_bf16(b)
    return _dot_nt(ah, bh) + (_dot_nt(ah, bl) + _dot_nt(al, bh))


def _dot_f32(a, b):
    ah, al = _split_bf16(a)
    bh, bl = _split_bf16(b)
    return _dot(ah, bh) + (_dot(ah, bl) + _dot(al, bh))


def _group_mean_sq(x, bd):
    hi, lo = _split_bf16(x * x)
    return _dot(hi, bd) + _dot(lo, bd)


def _lane(shape):
    return lax.broadcasted_iota(jnp.int32, shape, len(shape) - 1)


def _rotate_half(x, half):
    n = x.shape[-1]
    up = pltpu.roll(x, n - half, x.ndim - 1)
    down = pltpu.roll(x, half, x.ndim - 1)
    return jnp.where(_lane(x.shape) % (2 * half) < half, up, down)


def _rope(x, cos, sin_signed, half):
    return x * cos + _rotate_half(x, half) * sin_signed


def _row_rms(x):
    return x * lax.rsqrt(jnp.mean(x * x, axis=-1, keepdims=True) + EPS)


def _proj_kernel(x_ref, gattn_ref, win_ref, wq_ref, gains_ref, gqlat_ref, bd64_ref, bd32_ref,
                 cos64_ref, sin64_ref, cos32_ref, sin32_ref,
                 qm_ref, km_ref, vm_ref, qr_ref, kr_ref, vr_ref, gr_ref, ql_ref, ckv_ref, kpe_ref):
    x = x_ref[...]
    h = _row_rms(x) * gattn_ref[...]
    z = _dot(_bf(h), win_ref[...])
    bd64 = bd64_ref[...]
    bd32 = bd32_ref[...]
    cos64, sin64 = cos64_ref[...], sin64_ref[...]
    cos32, sin32 = cos32_ref[...], sin32_ref[...]
    g_mq, g_mk = gains_ref[0:1, :], gains_ref[1:2, :]
    g_kvlat, g_qn = gains_ref[2:3, :], gains_ref[3:4, :]
    g_qp, g_kp = gains_ref[4:5, :], gains_ref[5:6, :]

    def chunk(i):
        return z[:, i * LANES:(i + 1) * LANES]

    def norm_group(v, bd, g):
        return v * lax.rsqrt(_group_mean_sq(v, bd) + EPS) * g

    for i in range(3):
        qm_ref[:, i * LANES:(i + 1) * LANES] = _rope(norm_group(chunk(i), bd64, g_mq), cos64, sin64, 32)
    km_ref[...] = _rope(norm_group(chunk(3), bd64, g_mk), cos64, sin64, 32)
    vm_ref[...] = chunk(4)
    for i in range(2):
        qr_ref[:, i * LANES:(i + 1) * LANES] = _rope(chunk(5 + i), cos64, sin64, 32)
        kr_ref[:, i * LANES:(i + 1) * LANES] = _rope(chunk(7 + i), cos64, sin64, 32) * (RET_DK ** -0.5)
        vr_ref[:, i * LANES:(i + 1) * LANES] = chunk(9 + i)
        g = chunk(11 + i)
        gr_ref[:, i * LANES:(i + 1) * LANES] = g * jax.nn.sigmoid(g)
    zq = z[:, 13 * LANES:15 * LANES]
    ql = _dot(_bf(_row_rms(zq) * gqlat_ref[...]), wq_ref[...])
    for c in range(3):
        nope = ql[:, c * 256:c * 256 + LANES]
        pe = ql[:, c * 256 + LANES:(c + 1) * 256]
        ql_ref[:, c * 256:c * 256 + LANES] = norm_group(nope, bd64, g_qn)
        ql_ref[:, c * 256 + LANES:(c + 1) * 256] = _rope(norm_group(pe, bd32, g_qp), cos32, sin32, 16)
    ckv_ref[...] = _row_rms(chunk(15)) * g_kvlat
    kpe_ref[...] = _rope(norm_group(chunk(16), bd32, g_kp), cos32, sin32, 16)


def _in_proj_columns():
    cols = []
    for g in range(MOBA_GROUP):
        for hk in range(MOBA_KV_HEADS):
            h = hk * MOBA_GROUP + g
            cols += list(range(h * HEAD_DIM, (h + 1) * HEAD_DIM))
    cols += list(range(384, 2048))
    cols += list(range(2048, 2080)) * 2 + [-1] * 64
    return np.asarray(cols, np.int32)


def _q_up_columns():
    cols = []
    for c in range(MLA_HEADS // 2):
        for h in (2 * c, 2 * c + 1):
            cols += list(range(h * MLA_QK, h * MLA_QK + MLA_NOPE))
        for h in (2 * c, 2 * c + 1):
            cols += list(range(h * MLA_QK + MLA_NOPE, (h + 1) * MLA_QK))
        cols += [-1] * 64
    return np.asarray(cols, np.int32)


def _take_cols(w, cols):
    out = jnp.take(w, jnp.asarray(np.maximum(cols, 0)), axis=1)
    return out * jnp.asarray((cols >= 0).astype(np.float32))[None, :]


def _block_diag_avg(group):
    i = np.arange(LANES)
    return jnp.asarray(((i[:, None] // group) == (i[None, :] // group)).astype(np.float32) / group, BF16)


def _rope_tables(pos, dim):
    half = dim // 2
    inv = ROPE_THETA ** (-jnp.arange(half, dtype=F32) / half)
    ang = pos.astype(F32)[:, None] * inv[None, :]
    cos, sin = jnp.cos(ang), jnp.sin(ang)
    reps = LANES // dim
    return jnp.tile(jnp.concatenate([cos, cos], -1), (1, reps)), jnp.tile(jnp.concatenate([-sin, sin], -1), (1, reps))


def _tile_rows(n_rows, cap):
    t = min(cap, n_rows)
    assert n_rows % t == 0 and t % 8 == 0, (n_rows, t)
    return t


def _project(x2d, pos, n_pos_tiles, lw, tile):
    n, d = x2d.shape
    c64, s64 = _rope_tables(pos, 64)
    c32, s32 = _rope_tables(pos, 32)
    grid = (n // tile,)
    row = lambda w: pl.BlockSpec((tile, w), lambda i: (i, 0))
    const = lambda a: pl.BlockSpec(a.shape, lambda i: (0,) * a.ndim)
    tab = pl.BlockSpec((tile, LANES), lambda i: (i % n_pos_tiles, 0))
    widths = (384, 128, 128, 256, 256, 256, 256, 768, 128, 128)
    consts = (lw["g_attn"], lw["w_in"], lw["w_q_up"], lw["gains"], lw["g_qlat"], lw["bd64"], lw["bd32"])
    return pl.pallas_call(
        _proj_kernel,
        grid=grid,
        in_specs=[row(d)] + [const(a) for a in consts] + [tab] * 4,
        out_specs=[row(w) for w in widths],
        out_shape=[jax.ShapeDtypeStruct((n, w), F32) for w in widths],
        compiler_params=pltpu.CompilerParams(dimension_semantics=("arbitrary",), vmem_limit_bytes=VMEM_LIMIT),
        name="in_proj",
    )(x2d, *consts, c64, s64, c32, s32)


def _topk_mask(gate, n_valid, k):
    col = _lane(gate.shape)
    g = jnp.where(col < n_valid, gate, NEG_INF)
    sel = jnp.zeros(gate.shape, F32)
    big = jnp.int32(gate.shape[-1])
    for _ in range(k):
        m = jnp.max(g, axis=-1, keepdims=True)
        idx = jnp.min(jnp.where(g == m, col, big), axis=-1, keepdims=True)
        pick = (col == idx) & (m > NEG_INF)
        sel = jnp.where(pick, 1.0, sel)
        g = jnp.where(pick, NEG_INF, g)
    return sel


def _topk_mask_t(gate, n_valid, k):
    row = lax.broadcasted_iota(jnp.int32, gate.shape, 0)
    g = jnp.where(row < n_valid, gate, NEG_INF)
    sel = jnp.zeros(gate.shape, F32)
    big = jnp.int32(gate.shape[0])
    for _ in range(k):
        m = jnp.max(g, axis=0, keepdims=True)
        idx = jnp.min(jnp.where(g == m, row, big), axis=0, keepdims=True)
        pick = (row == idx) & (m > NEG_INF)
        sel = jnp.where(pick, 1.0, sel)
        g = jnp.where(pick, NEG_INF, g)
    return sel


def _moba_prompt_kernel(q_ref, k_ref, v_ref, o_ref, kb_sc, vt_sc, kmean_sc, qf_sc, qs_sc, sel_sc, m_sc, l_sc, acc_sc):
    tq = q_ref.shape[0]
    n_blk = kmean_sc.shape[0]
    qi = pl.program_id(1)

    @pl.when(qi == 0)
    def _():
        kb_sc[...] = _bf(k_ref[...])
        for j in range(n_blk):
            blk = slice(j * MOBA_BLOCK, (j + 1) * MOBA_BLOCK)
            vt_sc[j] = _bf(v_ref[blk, :].T)
            kmean_sc[j:j + 1, :] = jnp.mean(k_ref[blk, :], axis=0, keepdims=True)

    dim_row = lax.broadcasted_iota(jnp.int32, (LANES, tq), 0)
    for g in range(MOBA_GROUP):
        chunk_t = q_ref[:, g * LANES:(g + 1) * LANES].T
        for hk in range(MOBA_KV_HEADS):
            s = g * MOBA_KV_HEADS + hk
            qf_sc[:, s * tq:(s + 1) * tq] = jnp.where((dim_row // HEAD_DIM) == hk, chunk_t, 0.0)
    q_t = qf_sc[...]
    sel_sc[...] = _topk_mask_t(_dot_f32(kmean_sc[...], q_t), qi, MOBA_TOPK)
    qs_sc[...] = _bf(q_t * (HEAD_DIM ** -0.5))

    own = pl.multiple_of(qi * MOBA_BLOCK, MOBA_BLOCK)
    s = _dot(kb_sc[pl.ds(own, MOBA_BLOCK), :], qs_sc[...])
    kpos = lax.broadcasted_iota(jnp.int32, s.shape, 0)
    s = jnp.where(kpos <= _lane(s.shape) % tq, s, NEG_INF)
    m = jnp.max(s, axis=0, keepdims=True)
    p = jnp.exp(s - m)
    m_sc[...] = m
    l_sc[...] = jnp.sum(p, axis=0, keepdims=True)
    acc_sc[...] = _dot(vt_sc[qi], _bf(p))

    def past(j, carry):
        start = pl.multiple_of(j * MOBA_BLOCK, MOBA_BLOCK)
        chosen = sel_sc[pl.ds(j, 1), :] > 0.5
        s = jnp.where(chosen, _dot(kb_sc[pl.ds(start, MOBA_BLOCK), :], qs_sc[...]), NEG_INF)
        m_old = m_sc[...]
        m_new = jnp.maximum(m_old, jnp.max(s, axis=0, keepdims=True))
        alpha = jnp.exp(m_old - m_new)
        p = jnp.exp(s - m_new)
        m_sc[...] = m_new
        l_sc[...] = alpha * l_sc[...] + jnp.sum(p, axis=0, keepdims=True)
        acc_sc[...] = alpha * acc_sc[...] + _dot(vt_sc[j], _bf(p))
        return carry

    lax.fori_loop(0, qi, past, 0)

    o_t = acc_sc[...] / l_sc[...]
    for g in range(MOBA_GROUP):
        c0, c1 = (2 * g) * tq, (2 * g + 1) * tq
        pair_t = jnp.where(dim_row < HEAD_DIM, o_t[:, c0:c0 + tq], o_t[:, c1:c1 + tq])
        o_ref[:, g * LANES:(g + 1) * LANES] = pair_t.T


def _moba_prompt(q_m, k_m, v_m, batch, seq):
    assert seq % MOBA_BLOCK == 0
    tq = MOBA_BLOCK
    n_blk = seq // MOBA_BLOCK
    nq = seq // tq
    rows = MOBA_HEADS * tq
    return pl.pallas_call(
        _moba_prompt_kernel,
        grid=(batch, nq),
        in_specs=[pl.BlockSpec((tq, 384), lambda b, i: (b * nq + i, 0)),
                  pl.BlockSpec((seq, LANES), lambda b, i: (b, 0)),
                  pl.BlockSpec((seq, LANES), lambda b, i: (b, 0))],
        out_specs=pl.BlockSpec((tq, 384), lambda b, i: (b * nq + i, 0)),
        out_shape=jax.ShapeDtypeStruct((batch * seq, 384), F32),
        scratch_shapes=[pltpu.VMEM((seq, LANES), BF16), pltpu.VMEM((n_blk, LANES, MOBA_BLOCK), BF16),
                        pltpu.VMEM((n_blk, LANES), F32), pltpu.VMEM((LANES, rows), F32),
                        pltpu.VMEM((LANES, rows), BF16), pltpu.VMEM((n_blk, rows), F32),
                        pltpu.VMEM((1, rows), F32), pltpu.VMEM((1, rows), F32), pltpu.VMEM((LANES, rows), F32)],
        compiler_params=pltpu.CompilerParams(dimension_semantics=("arbitrary", "arbitrary"),
                                             vmem_limit_bytes=VMEM_LIMIT),
        name="moba_prompt",
    )(q_m, k_m, v_m)


def _mla_prompt_kernel(q_ref, ckv_ref, kpe_ref, wkv_ref, gkn_ref, bd64_ref, o_ref,
                       kcat_sc, vt_sc, qs_sc, m_sc, l_sc, acc_sc):
    tq = q_ref.shape[0]
    seq = ckv_ref.shape[0]
    n_pairs = MLA_HEADS // 2
    qi = pl.program_id(1)

    @pl.when(qi == 0)
    def _():
        def expand(t, carry):
            r = pl.ds(pl.multiple_of(t * tq, tq), tq)
            kv = _dot(_bf(ckv_ref[r, :]), wkv_ref[...])
            pe = _bf(kpe_ref[r, :])
            for c in range(n_pairs):
                kn = kv[:, c * LANES:(c + 1) * LANES]
                kn = kn * lax.rsqrt(_group_mean_sq(kn, bd64_ref[...]) + EPS) * gkn_ref[...]
                kcat_sc[c, r, 0:LANES] = _bf(kn)
                kcat_sc[c, r, LANES:2 * LANES] = pe
                vt_sc[c, t] = _bf(kv[:, (n_pairs + c) * LANES:(n_pairs + c + 1) * LANES].T)
            return carry
        lax.fori_loop(0, seq // tq, expand, 0)

    depth = lax.broadcasted_iota(jnp.int32, (256, tq), 0)
    for c in range(n_pairs):
        chunk_t = (q_ref[:, c * 256:(c + 1) * 256] * (MLA_QK ** -0.5)).T
        for hh in range(2):
            nope = (depth < LANES) & ((depth // MLA_NOPE) == hh)
            pe = (depth >= LANES) & (depth < LANES + 2 * MLA_ROPE) & (((depth - LANES) // MLA_ROPE) == hh)
            qs_sc[c, :, hh * tq:(hh + 1) * tq] = _bf(jnp.where(nope | pe, chunk_t, 0.0))

    own = pl.multiple_of(qi * tq, tq)
    for c in range(n_pairs):
        s = _dot(kcat_sc[c, pl.ds(own, tq), :], qs_sc[c])
        kpos = lax.broadcasted_iota(jnp.int32, s.shape, 0)
        s = jnp.where(kpos <= _lane(s.shape) % tq, s, NEG_INF)
        m = jnp.max(s, axis=0, keepdims=True)
        p = jnp.exp(s - m)
        m_sc[c] = m
        l_sc[c] = jnp.sum(p, axis=0, keepdims=True)
        acc_sc[c] = _dot(vt_sc[c, qi], _bf(p))

    def past(j, carry):
        start = pl.multiple_of(j * tq, tq)
        for c in range(n_pairs):
            s = _dot(kcat_sc[c, pl.ds(start, tq), :], qs_sc[c])
            m_old = m_sc[c]
            m_new = jnp.maximum(m_old, jnp.max(s, axis=0, keepdims=True))
            alpha = jnp.exp(m_old - m_new)
            p = jnp.exp(s - m_new)
            m_sc[c] = m_new
            l_sc[c] = alpha * l_sc[c] + jnp.sum(p, axis=0, keepdims=True)
            acc_sc[c] = alpha * acc_sc[c] + _dot(vt_sc[c, j], _bf(p))
        return carry

    lax.fori_loop(0, qi, past, 0)

    dim_row = lax.broadcasted_iota(jnp.int32, (LANES, tq), 0)
    for c in range(n_pairs):
        o_t = acc_sc[c] / l_sc[c]
        o_ref[:, c * LANES:(c + 1) * LANES] = jnp.where(dim_row < MLA_V, o_t[:, 0:tq], o_t[:, tq:2 * tq]).T


def _mla_prompt(q_l, c_kv, kpe_slot, lw, batch, seq):
    tq = 256
    assert seq % tq == 0
    nq = seq // tq
    n_pairs = MLA_HEADS // 2
    const = lambda a: pl.BlockSpec(a.shape, lambda b, i: (0,) * a.ndim)
    return pl.pallas_call(
        _mla_prompt_kernel,
        grid=(batch, nq),
        in_specs=[pl.BlockSpec((tq, 768), lambda b, i: (b * nq + i, 0)),
                  pl.BlockSpec((seq, LANES), lambda b, i: (b, 0)),
                  pl.BlockSpec((seq, LANES), lambda b, i: (b, 0)),
                  const(lw["w_kv_up"]), const(lw["g_kn"]), const(lw["bd64"])],
        out_specs=pl.BlockSpec((tq, 384), lambda b, i: (b * nq + i, 0)),
        out_shape=jax.ShapeDtypeStruct((batch * seq, 384), F32),
        scratch_shapes=[pltpu.VMEM((n_pairs, seq, 256), BF16), pltpu.VMEM((n_pairs, nq, LANES, tq), BF16),
                        pltpu.VMEM((n_pairs, 256, 2 * tq), BF16), pltpu.VMEM((n_pairs, 1, 2 * tq), F32),
                        pltpu.VMEM((n_pairs, 1, 2 * tq), F32), pltpu.VMEM((n_pairs, LANES, 2 * tq), F32)],
        compiler_params=pltpu.CompilerParams(dimension_semantics=("arbitrary", "arbitrary"),
                                             vmem_limit_bytes=VMEM_LIMIT),
        name="mla_prompt",
    )(q_l, c_kv, kpe_slot, lw["w_kv_up"], lw["g_kn"], lw["bd64"])


def _retention_kernel(q_ref, k_ref, v_ref, gate_ref, st_ref, dec_ref, gpow_ref, wts_ref, gc_ref, bdm_ref,
                      gout_ref, bd64_ref, o_ref, sto_ref, st_sc):
    chunk = dec_ref.shape[-1]
    n_seqs, n_pairs = st_ref.shape[0], st_ref.shape[1]
    seq = q_ref.shape[0] // n_seqs
    lane = _lane((chunk, LANES))

    for i in range(n_seqs):
        for p in range(n_pairs):
            st_sc[p] = st_ref[i, p]

        def step(t, carry):
            r = pl.ds(pl.multiple_of(i * seq + t * chunk, chunk), chunk)
            for p in range(n_pairs):
                cols = slice(p * LANES, (p + 1) * LANES)
                q, k, v = q_ref[r, cols], k_ref[r, cols], v_ref[r, cols]
                kb, vb = _bf(k), _bf(v)
                halves = []
                for hh in range(2):
                    qh = _bf(jnp.where((lane // RET_DK) == hh, q, 0.0))
                    s = _dot_nt(qh, kb) * dec_ref[p, hh]
                    halves.append(_dot(_bf(s), vb))
                o = jnp.where(lane < RET_DV, halves[0], halves[1])
                o = o + _dot(_bf(q), _bf(st_sc[p])) * gpow_ref[p]
                u = _dot(_bf((k * wts_ref[p]).T), vb)
                st_sc[p] = st_sc[p] * gc_ref[p] + u * bdm_ref[...]
                o = o * lax.rsqrt(_group_mean_sq(o, bd64_ref[...]) + EPS) * gout_ref[...]
                o_ref[r, cols] = o * gate_ref[r, cols]
            return carry

        lax.fori_loop(0, seq // chunk, step, 0)
        for p in range(n_pairs):
            sto_ref[i, p] = st_sc[p]


def _retention_tables(chunk):
    lg = jnp.log(1.0 - 2.0 ** (-5.0 - jnp.arange(RET_HEADS, dtype=F32)))
    idx = jnp.arange(chunk, dtype=F32)
    diff = idx[:, None] - idx[None, :]
    decay = jnp.where(diff >= 0, jnp.exp(lg[:, None, None] * jnp.maximum(diff, 0.0)), 0.0)
    gpow = jnp.exp(lg[None, :] * (idx[:, None] + 1.0))
    wts = jnp.exp(lg[:, None] * (chunk - 1.0 - idx)[None, :]).T
    gc = jnp.exp(lg * chunk)
    n_pairs = RET_HEADS // 2
    lanes = lambda t: jnp.repeat(t.reshape(chunk, n_pairs, 2), RET_DV, axis=2).transpose(1, 0, 2)
    i = np.arange(LANES)
    bdm = jnp.asarray(((i[:, None] // RET_DK) == (i[None, :] // RET_DV)).astype(np.float32))
    gc_l = jnp.repeat(gc.reshape(n_pairs, 2), RET_DK, axis=1)
    gc_t = gc_l[:, :, None] * bdm[None]
    return decay.reshape(n_pairs, 2, chunk, chunk), lanes(gpow), lanes(wts), gc_t, bdm


def _retention(q_r, k_r, v_r, gate_r, state_bd, lw, batch, seq):
    chunk = min(RET_CHUNK, seq)
    assert seq % chunk == 0
    n_pairs = RET_HEADS // 2
    dec, gpow, wts, gc_t, bdm = _retention_tables(chunk)
    per_step = 8 if (seq == chunk and batch % 8 == 0) else 1
    act = pl.BlockSpec((per_step * seq, n_pairs * LANES), lambda b: (b, 0))
    st = pl.BlockSpec((per_step, n_pairs, LANES, LANES), lambda b: (b, 0, 0, 0))
    const = lambda a: pl.BlockSpec(a.shape, lambda b: (0,) * a.ndim)
    return pl.pallas_call(
        _retention_kernel,
        grid=(batch // per_step,),
        in_specs=[act, act, act, act, st, const(dec), const(gpow), const(wts), const(gc_t),
                  const(bdm), const(lw["g_ret_out"]), const(lw["bd64"])],
        out_specs=[act, st],
        out_shape=[jax.ShapeDtypeStruct((batch * seq, n_pairs * LANES), F32),
                   jax.ShapeDtypeStruct((batch, n_pairs, LANES, LANES), F32)],
        scratch_shapes=[pltpu.VMEM((n_pairs, LANES, LANES), F32)],
        compiler_params=pltpu.CompilerParams(dimension_semantics=("arbitrary",),
                                             vmem_limit_bytes=VMEM_LIMIT),
        name="retention",
    )(q_r, k_r, v_r, gate_r, state_bd, dec, gpow, wts, gc_t, bdm, lw["g_ret_out"], lw["bd64"])


def _state_to_block_diag(st):
    b = st.shape[0]
    s = st.reshape(b, RET_HEADS // 2, 2, RET_DK, RET_DV)
    eye = jnp.eye(2, dtype=st.dtype)
    return jnp.einsum("bpird,ij->bpirjd", s, eye).reshape(b, RET_HEADS // 2, 2 * RET_DK, 2 * RET_DV)


def _block_diag_to_state(bd):
    b = bd.shape[0]
    s = bd.reshape(b, RET_HEADS // 2, 2, RET_DK, 2, RET_DV)
    return jnp.stack([s[:, :, 0, :, 0, :], s[:, :, 1, :, 1, :]], axis=2).reshape(b, RET_HEADS, RET_DK, RET_DV)


def _out_mlp_kernel(x_ref, om_ref, or_ref, ol_ref, wom_ref, wor_ref, wol_ref, gmlp_ref, wup_ref, wdn_ref, y_ref):
    x1 = x_ref[...] + (_dot(_bf(om_ref[...]), wom_ref[...]) + _dot(_bf(or_ref[...]), wor_ref[...])
                       + _dot(_bf(ol_ref[...]), wol_ref[...]))
    h = _bf(_row_rms(x1) * gmlp_ref[...])
    d_ff = wup_ref.shape[1]
    step = min(1024, d_ff)
    acc = x1
    for c in range(d_ff // step):
        u = jnp.maximum(_dot(h, wup_ref[:, c * step:(c + 1) * step]), 0.0)
        acc = acc + _dot(_bf(u * u), wdn_ref[c * step:(c + 1) * step, :])
    y_ref[...] = acc


def _out_mlp(x2d, o_m, o_r, o_l, lw, tile):
    n, d = x2d.shape
    row = lambda w: pl.BlockSpec((tile, w), lambda i: (i, 0))
    const = lambda a: pl.BlockSpec(a.shape, lambda i: (0,) * a.ndim, pipeline_mode=pl.Buffered(1))
    consts = (lw["w_out_m"], lw["w_out_r"], lw["w_out_l"], lw["g_mlp"], lw["w_mlp_up"], lw["w_mlp_down"])
    return pl.pallas_call(
        _out_mlp_kernel,
        grid=(n // tile,),
        in_specs=[row(d), row(384), row(256), row(384)] + [const(a) for a in consts],
        out_specs=row(d),
        out_shape=jax.ShapeDtypeStruct((n, d), F32),
        compiler_params=pltpu.CompilerParams(dimension_semantics=("arbitrary",), vmem_limit_bytes=VMEM_LIMIT),
        name="out_mlp",
    )(x2d, o_m, o_r, o_l, *consts)


def _layer_weights(l, g_attn, w_in, g_moba_q, g_moba_k, g_ret_out, g_mla_qlat, w_mla_q_up, g_mla_qn, g_mla_qp,
                   g_mla_kvlat, w_mla_kv_up, g_mla_kn, g_mla_kp, w_out, g_mlp, w_mlp_up, w_mlp_down):
    ones64 = jnp.ones((64,), F32)
    gains = jnp.stack([
        jnp.tile(g_moba_q[l], 2), jnp.tile(g_moba_k[l], 2), g_mla_kvlat[l], jnp.tile(g_mla_qn[l], 2),
        jnp.concatenate([g_mla_qp[l], g_mla_qp[l], ones64]), jnp.concatenate([g_mla_kp[l], g_mla_kp[l], ones64]),
        jnp.ones((LANES,), F32), jnp.ones((LANES,), F32)])
    kv = w_mla_kv_up[l].reshape(MLA_KV_LORA, MLA_HEADS, MLA_NOPE + MLA_V)
    w_kn = kv[:, :, :MLA_NOPE].reshape(MLA_KV_LORA, MLA_HEADS * MLA_NOPE)
    w_v = kv[:, :, MLA_NOPE:].reshape(MLA_KV_LORA, MLA_HEADS * MLA_V)
    wo = w_out[l]
    perm = [(hk * MOBA_GROUP + g) for g in range(MOBA_GROUP) for hk in range(MOBA_KV_HEADS)]
    wo_m = wo[:MOBA_HEADS * HEAD_DIM].reshape(MOBA_HEADS, HEAD_DIM, -1)[jnp.asarray(perm)].reshape(MOBA_HEADS * HEAD_DIM, -1)
    r0 = MOBA_HEADS * HEAD_DIM
    r1 = r0 + RET_HEADS * RET_DV
    return {
        "g_attn": g_attn[l][None, :],
        "w_in": _bf(_take_cols(w_in[l], _in_proj_columns())),
        "w_q_up": _bf(_take_cols(w_mla_q_up[l], _q_up_columns())),
        "gains": gains,
        "g_qlat": g_mla_qlat[l][None, :],
        "bd64": _block_diag_avg(64),
        "bd32": _block_diag_avg(32),
        "w_kv_up": _bf(jnp.concatenate([w_kn, w_v], axis=1)),
        "w_kn_f32": w_kn,
        "w_v": _bf(w_v),
        "g_kn": jnp.tile(g_mla_kn[l], 2)[None, :],
        "g_ret_out": jnp.tile(g_ret_out[l], 2)[None, :],
        "w_out_m": _bf(wo_m), "w_out_r": _bf(wo[r0:r1]), "w_out_l": _bf(wo[r1:]),
        "g_mlp": g_mlp[l][None, :],
        "w_mlp_up": _bf(w_mlp_up[l]), "w_mlp_down": _bf(w_mlp_down[l]),
    }


def _rows_of_page(page):
    return lambda buf, slot, pg: buf.at[slot, pl.ds(pg * page, page), :]


def _whole_page(buf, slot, pg):
    return buf.at[slot, pg]


def _page_copies(pt_ref, layer, req, chunk, slot, pages_per_chunk, streams):
    copies = []
    for hbm, buf, sem, window in streams:
        for pg in range(pages_per_chunk):
            pid = pt_ref[req, chunk * pages_per_chunk + pg]
            copies.append(pltpu.make_async_copy(hbm.at[layer, pid], window(buf, slot, pg), sem.at[slot]))
    return copies


GATHER_AHEAD = 3
GATHER_SLOTS = GATHER_AHEAD + 1


def _gather_first(pt_ref, layer, n_ch, pages_per_chunk, streams):
    assert n_ch >= GATHER_AHEAD

    @pl.when(pl.program_id(0) == 0)
    def _():
        for s in range(GATHER_AHEAD):
            for c in _page_copies(pt_ref, layer, 0, s, s, pages_per_chunk, streams):
                c.start()


def _gather_chunk(pt_ref, layer, ch, n_ch, pages_per_chunk, streams):
    r, n_req = pl.program_id(0), pl.num_programs(0)
    step = r * n_ch + ch
    slot = step % GATHER_SLOTS
    ahead = ch + GATHER_AHEAD
    wrap = ahead >= n_ch
    nr = jnp.where(wrap, jnp.minimum(r + 1, n_req - 1), r)
    nc = jnp.where(wrap, ahead - n_ch, ahead)
    for c in _page_copies(pt_ref, layer, nr, nc, (step + GATHER_AHEAD) % GATHER_SLOTS, pages_per_chunk, streams):
        c.start()
    for c in _page_copies(pt_ref, layer, r, ch, slot, pages_per_chunk, streams):
        c.wait()
    return slot


def _gather_drain(pt_ref, layer, n_ch, pages_per_chunk, streams):
    r, n_req = pl.program_id(0), pl.num_programs(0)

    @pl.when(r == n_req - 1)
    def _():
        for i in range(GATHER_AHEAD):
            slot = (n_req * n_ch + i) % GATHER_SLOTS
            for c in _page_copies(pt_ref, layer, r, i, slot, pages_per_chunk, streams):
                c.wait()


def _mla_decode_kernel(layer, n_ch, pages_per_chunk, page, sub,
                       pt_ref, lat_hbm, kpe_hbm, qn_ref, qpe_ref, cnew_ref, penew_ref, wkt_ref, gkn_ref,
                       wv_ref, o_ref, cbuf, pbuf, sem_c, sem_p, lhs_sc, cb_sc, s_sc, m_sc, l_sc, acc_sc):
    n_dec = cnew_ref.shape[0]
    rows = MLA_HEADS * n_dec
    n_y = MLA_HEADS * MLA_NOPE
    streams = ((lat_hbm, cbuf, sem_c, _rows_of_page(page)), (kpe_hbm, pbuf, sem_p, _whole_page))
    _gather_first(pt_ref, layer, n_ch, pages_per_chunk, streams)

    lhs_sc[0:n_y, :] = wkt_ref[...]
    for h in range(MLA_HEADS):
        qn = qn_ref[0, h * n_dec:(h + 1) * n_dec, :] * gkn_ref[...]
        lhs_sc[n_y + h * n_dec:n_y + (h + 1) * n_dec, :] = _dot(
            _bf(qn), _bf(wkt_ref[h * MLA_NOPE:(h + 1) * MLA_NOPE, :]))
    m_sc[...] = jnp.full(m_sc.shape, NEG_INF, F32)
    l_sc[...] = jnp.zeros(l_sc.shape, F32)
    acc_sc[...] = jnp.zeros(acc_sc.shape, F32)

    lhs = _bf(lhs_sc[...])
    qpe = _bf(qpe_ref[0])
    scale = MLA_QK ** -0.5

    def scores(cb, s_pe):
        r = _dot_nt(lhs, cb)
        per_head = []
        for h in range(MLA_HEADS):
            y = r[h * MLA_NOPE:(h + 1) * MLA_NOPE, :]
            rinv = lax.rsqrt(jnp.mean(y * y, axis=0, keepdims=True) + EPS)
            per_head.append(r[n_y + h * n_dec:n_y + (h + 1) * n_dec, :] * rinv)
        return (jnp.concatenate(per_head, axis=0) + s_pe) * scale

    def update(s, cb):
        m_old = m_sc[...]
        m_new = jnp.maximum(m_old, jnp.max(s, axis=-1, keepdims=True))
        alpha = jnp.exp(m_old - m_new)
        p = jnp.exp(s - m_new)
        m_sc[...] = m_new
        l_sc[...] = alpha * l_sc[...] + jnp.sum(p, axis=-1, keepdims=True)
        acc_sc[...] = alpha * acc_sc[...] + _dot(_bf(p), cb)

    def chunk(ch, carry):
        slot = _gather_chunk(pt_ref, layer, ch, n_ch, pages_per_chunk, streams)
        s_pe = _dot(qpe, _bf(jnp.concatenate([pbuf[slot, pg] for pg in range(pages_per_chunk)], axis=1)))
        for t in range((pages_per_chunk * page) // sub):
            rws = slice(t * sub, (t + 1) * sub)
            cb = _bf(cbuf[slot, rws, :])
            cb_sc[rws, :] = cb
            s_sc[:, rws] = scores(cb, s_pe[:, rws])
        update(s_sc[...], cb_sc[...])
        return carry

    lax.fori_loop(0, n_ch, chunk, 0)

    cb = _bf(cnew_ref[...])
    s = scores(cb, _dot_nt(qpe, _bf(penew_ref[0])))
    t_row = lax.broadcasted_iota(jnp.int32, s.shape, 0) % n_dec
    update(jnp.where(_lane(s.shape) <= t_row, s, NEG_INF), cb)
    o_lat = _bf(acc_sc[...] / l_sc[...])
    lane = _lane((n_dec, LANES))
    for c in range(MLA_HEADS // 2):
        wv = wv_ref[:, c * LANES:(c + 1) * LANES]
        o0 = _dot(o_lat[(2 * c) * n_dec:(2 * c + 1) * n_dec, :], wv)
        o1 = _dot(o_lat[(2 * c + 1) * n_dec:(2 * c + 2) * n_dec, :], wv)
        o_ref[:, c * LANES:(c + 1) * LANES] = jnp.where(lane < MLA_V, o0, o1)
    _gather_drain(pt_ref, layer, n_ch, pages_per_chunk, streams)


def _decode_chunk_keys(past_len, page):
    keys = min(2048, max(MOBA_BLOCK, past_len // 8))
    assert past_len % keys == 0 and keys % page == 0, (past_len, keys, page)
    return keys


def _mla_decode(layer, page_table, lat_cache, kpe_cache, q_l, c_new, kpe_slot, lw, n_req, n_dec):
    page = lat_cache.shape[2]
    past_len = page_table.shape[1] * page
    keys = _decode_chunk_keys(past_len, page)
    sub = min(256, keys)
    rows = MLA_HEADS * n_dec
    arr = q_l.reshape(n_req, n_dec, MLA_HEADS // 2, 256)
    heads_first = lambda a, w: a.reshape(n_req, n_dec, MLA_HEADS // 2, 2, w).transpose(0, 2, 3, 1, 4).reshape(n_req, rows, w)
    qn_rows = heads_first(arr[..., :LANES], MLA_NOPE)
    qpe_rows = heads_first(arr[..., LANES:LANES + 2 * MLA_ROPE], MLA_ROPE)
    pe_new = kpe_slot[:, :MLA_ROPE].reshape(n_req, n_dec, MLA_ROPE)
    assert sub % page == 0 and page == LANES
    w_kn_t = lw["w_kn_f32"].T
    g_kn = lw["g_kn"][:, :MLA_NOPE]
    const = lambda a: pl.BlockSpec(a.shape, lambda r, pt: (0,) * a.ndim)
    kern = functools.partial(_mla_decode_kernel, layer, past_len // keys, keys // page, page, sub)
    return pl.pallas_call(
        kern,
        grid_spec=pltpu.PrefetchScalarGridSpec(
            num_scalar_prefetch=1,
            grid=(n_req,),
            in_specs=[pl.BlockSpec(memory_space=pl.ANY), pl.BlockSpec(memory_space=pl.ANY),
                      pl.BlockSpec((1, rows, MLA_NOPE), lambda r, pt: (r, 0, 0)),
                      pl.BlockSpec((1, rows, MLA_ROPE), lambda r, pt: (r, 0, 0)),
                      pl.BlockSpec((n_dec, LANES), lambda r, pt: (r, 0)),
                      pl.BlockSpec((1, n_dec, MLA_ROPE), lambda r, pt: (r, 0, 0)),
                      const(w_kn_t), const(g_kn), const(lw["w_v"])],
            out_specs=pl.BlockSpec((n_dec, MLA_HEADS * MLA_V), lambda r, pt: (r, 0)),
            scratch_shapes=[pltpu.VMEM((GATHER_SLOTS, keys, LANES), F32),
                            pltpu.VMEM((GATHER_SLOTS, keys // page, MLA_ROPE, page), F32),
                            pltpu.SemaphoreType.DMA((GATHER_SLOTS,)), pltpu.SemaphoreType.DMA((GATHER_SLOTS,)),
                            pltpu.VMEM((MLA_HEADS * MLA_NOPE + rows, LANES), F32),
                            pltpu.VMEM((keys, LANES), BF16), pltpu.VMEM((rows, keys), F32),
                            pltpu.VMEM((rows, 1), F32), pltpu.VMEM((rows, 1), F32), pltpu.VMEM((rows, LANES), F32)]),
        out_shape=jax.ShapeDtypeStruct((n_req * n_dec, MLA_HEADS * MLA_V), F32),
        compiler_params=pltpu.CompilerParams(dimension_semantics=("arbitrary",),
                                             vmem_limit_bytes=VMEM_LIMIT),
        name="mla_decode",
    )(page_table, lat_cache, kpe_cache.transpose(0, 1, 3, 2), qn_rows, qpe_rows, c_new, pe_new, w_kn_t, g_kn,
      lw["w_v"])


def _moba_decode_kernel(layer, n_ch, pages_per_chunk, page,
                        pt_ref, k_hbm, v_hbm, q_ref, knew_ref, vnew_ref, o_ref,
                        kbuf, vbuf, sem_k, sem_v, vall_sc, s_sc):
    n_dec = knew_ref.shape[0]
    keys = pages_per_chunk * page
    blocks_per_chunk = keys // MOBA_BLOCK
    n_blk = n_ch * blocks_per_chunk
    pages_per_block = MOBA_BLOCK // page
    streams = ((k_hbm, kbuf, sem_k, _whole_page), (v_hbm, vbuf, sem_v, _whole_page))
    _gather_first(pt_ref, layer, n_ch, pages_per_chunk, streams)
    q = q_ref[0]
    qb = _bf(q * (HEAD_DIM ** -0.5))
    rows = q.shape[0]
    blk_col = _lane((page, n_blk))

    def chunk(ch, carry):
        kmean, bmax = carry
        slot = _gather_chunk(pt_ref, layer, ch, n_ch, pages_per_chunk, streams)
        k_pages = [kbuf[slot, pg].reshape(LANES, page) for pg in range(pages_per_chunk)]
        s = _dot(qb, _bf(jnp.concatenate(k_pages, axis=1)))
        s_sc[ch] = s
        for b in range(blocks_per_chunk):
            best = s[:, b * MOBA_BLOCK:b * MOBA_BLOCK + page]
            for i in range(1, pages_per_block):
                best = jnp.maximum(best, s[:, b * MOBA_BLOCK + i * page:b * MOBA_BLOCK + (i + 1) * page])
            bmax = jnp.where(_lane(bmax.shape) == ch * blocks_per_chunk + b,
                             jnp.max(best, axis=-1, keepdims=True), bmax)
        for pg in range(pages_per_chunk):
            vall_sc[ch, :, pg * page:(pg + 1) * page] = _bf(vbuf[slot, pg].reshape(LANES, page))
        for b in range(blocks_per_chunk):
            ksum = k_pages[b * pages_per_block]
            for i in range(1, pages_per_block):
                ksum = ksum + k_pages[b * pages_per_block + i]
            hi, lo = _split_bf16(ksum)
            pick = _bf(jnp.where(blk_col == ch * blocks_per_chunk + b, 1.0 / MOBA_BLOCK, 0.0))
            kmean = kmean + (_dot(hi, pick) + _dot(lo, pick))
        return kmean, bmax

    kmean, bmax = lax.fori_loop(0, n_ch, chunk,
                                (jnp.zeros((LANES, n_blk), F32), jnp.full((rows, n_blk), NEG_INF, F32)))

    sel_f = _topk_mask(_dot_f32(q, kmean), n_blk, MOBA_TOPK)
    sel = _bf(sel_f)
    s_own = _dot_nt(qb, _bf(knew_ref[...]))
    t_row = lax.broadcasted_iota(jnp.int32, s_own.shape, 0) % n_dec
    s_own = jnp.where(_lane(s_own.shape) <= t_row, s_own, NEG_INF)
    blk_of_key = (lax.broadcasted_iota(jnp.int32, (n_blk, keys), 1) // MOBA_BLOCK
                  - lax.broadcasted_iota(jnp.int32, (n_blk, keys), 0))

    def chunk_mask(c):
        return _dot(sel, _bf(jnp.where(blk_of_key + c * blocks_per_chunk == 0, 1.0, 0.0))) > 0.5

    def fold(x, op):
        out = x[:, 0:page]
        for i in range(1, keys // page):
            out = op(out, x[:, i * page:(i + 1) * page])
        return out

    m_sel = jnp.max(jnp.where(sel_f > 0.5, bmax, NEG_INF), axis=-1, keepdims=True)
    m = jnp.maximum(m_sel, jnp.max(s_own, axis=-1, keepdims=True))

    def chunk_pv(c, carry):
        lsum, acc = carry
        p = jnp.where(chunk_mask(c), jnp.exp(s_sc[c] - m), 0.0)
        return lsum + fold(p, jnp.add), acc + _dot_nt(_bf(p), vall_sc[c])

    p_own = jnp.exp(s_own - m)
    lsum, acc = lax.fori_loop(0, n_ch, chunk_pv,
                              (jnp.zeros((rows, page), F32), _dot(_bf(p_own), _bf(vnew_ref[...]))),
                              unroll=min(4, n_ch))
    l = jnp.sum(lsum, axis=-1, keepdims=True) + jnp.sum(p_own, axis=-1, keepdims=True)
    o_ref[0] = acc / l
    _gather_drain(pt_ref, layer, n_ch, pages_per_chunk, streams)


def _moba_decode(layer, page_table, k_cache, v_cache, q_m, k_new, v_new, n_req, n_dec):
    page = k_cache.shape[2]
    past_len = page_table.shape[1] * page
    assert past_len % MOBA_BLOCK == 0 and MOBA_BLOCK % page == 0
    keys = _decode_chunk_keys(past_len, page)
    assert keys % MOBA_BLOCK == 0
    n_blk = past_len // MOBA_BLOCK
    rows = MOBA_HEADS * n_dec
    qg = q_m.reshape(n_req, n_dec, MOBA_GROUP, LANES).transpose(0, 2, 1, 3)
    half = (np.arange(LANES) // HEAD_DIM)[None, :] == np.arange(MOBA_KV_HEADS)[:, None]
    q_rows = (qg[:, None] * jnp.asarray(half.astype(np.float32))[None, :, None, None, :]).reshape(n_req, rows, LANES)
    k_t = k_cache.transpose(0, 1, 3, 4, 2)
    v_t = v_cache.transpose(0, 1, 3, 4, 2)
    page_buf = pltpu.VMEM((GATHER_SLOTS, keys // page, MOBA_KV_HEADS, HEAD_DIM, page), F32)
    kern = functools.partial(_moba_decode_kernel, layer, past_len // keys, keys // page, page)
    o_rows = pl.pallas_call(
        kern,
        grid_spec=pltpu.PrefetchScalarGridSpec(
            num_scalar_prefetch=1,
            grid=(n_req,),
            in_specs=[pl.BlockSpec(memory_space=pl.ANY), pl.BlockSpec(memory_space=pl.ANY),
                      pl.BlockSpec((1, rows, LANES), lambda r, pt: (r, 0, 0)),
                      pl.BlockSpec((n_dec, LANES), lambda r, pt: (r, 0)),
                      pl.BlockSpec((n_dec, LANES), lambda r, pt: (r, 0))],
            out_specs=pl.BlockSpec((1, rows, LANES), lambda r, pt: (r, 0, 0)),
            scratch_shapes=[page_buf, page_buf,
                            pltpu.SemaphoreType.DMA((GATHER_SLOTS,)), pltpu.SemaphoreType.DMA((GATHER_SLOTS,)),
                            pltpu.VMEM((past_len // keys, LANES, keys), BF16),
                            pltpu.VMEM((past_len // keys, rows, keys), F32)]),
        out_shape=jax.ShapeDtypeStruct((n_req, rows, LANES), F32),
        compiler_params=pltpu.CompilerParams(dimension_semantics=("arbitrary",),
                                             vmem_limit_bytes=VMEM_LIMIT),
        name="moba_decode",
    )(page_table, k_t, v_t, q_rows, k_new, v_new)
    o = o_rows.reshape(n_req, MOBA_KV_HEADS, MOBA_GROUP, n_dec, LANES)
    pairs = jnp.concatenate([o[:, 0, :, :, :HEAD_DIM], o[:, 1, :, :, HEAD_DIM:]], axis=-1)
    return pairs.transpose(0, 2, 1, 3).reshape(n_req * n_dec, MOBA_GROUP * LANES)


def _layer_sample(layer, x2d, lw, caches, state, page_table, n_req, n_dec):
    ck, cv, clat, ckpe = caches
    past_len = page_table.shape[1] * ck.shape[2]
    n = n_req * n_dec
    tile = _tile_rows(n, 256)
    assert tile % n_dec == 0
    pos = jnp.tile(past_len + jnp.arange(n_dec), tile // n_dec)
    q_m, k_m, v_m, q_r, k_r, v_r, gate_r, q_l, c_kv, kpe = _project(x2d, pos, 1, lw, tile)
    o_m = _moba_decode(layer, page_table, ck, cv, q_m, k_m, v_m, n_req, n_dec)
    o_r, st = _retention(q_r, k_r, v_r, gate_r, _state_to_block_diag(state.astype(F32)), lw, n_req, n_dec)
    o_l = _mla_decode(layer, page_table, clat, ckpe, q_l, c_kv, kpe, lw, n_req, n_dec)
    y = _out_mlp(x2d, o_m, o_r, o_l, lw, _tile_rows(n, 512))
    new = (k_m.reshape(n_req, n_dec, MOBA_KV_HEADS, HEAD_DIM), v_m.reshape(n_req, n_dec, MOBA_KV_HEADS, HEAD_DIM),
           c_kv.reshape(n_req, n_dec, MLA_KV_LORA), kpe[:, :MLA_ROPE].reshape(n_req, n_dec, MLA_ROPE),
           _block_diag_to_state(st).astype(state.dtype))
    return y, new


def _layer_prompt(x2d, lw, batch, seq):
    tile = _tile_rows(seq, 256)
    q_m, k_m, v_m, q_r, k_r, v_r, gate_r, q_l, c_kv, kpe = _project(
        x2d, jnp.arange(seq), seq // tile, lw, tile)
    o_m = _moba_prompt(q_m, k_m, v_m, batch, seq)
    st0 = jnp.zeros((batch, RET_HEADS // 2, LANES, LANES), F32)
    o_r, st = _retention(q_r, k_r, v_r, gate_r, st0, lw, batch, seq)
    o_l = _mla_prompt(q_l, c_kv, kpe, lw, batch, seq)
    y = _out_mlp(x2d, o_m, o_r, o_l, lw, _tile_rows(x2d.shape[0], 512))
    new = (k_m.reshape(batch, seq, MOBA_KV_HEADS, HEAD_DIM), v_m.reshape(batch, seq, MOBA_KV_HEADS, HEAD_DIM),
           c_kv.reshape(batch, seq, MLA_KV_LORA), kpe[:, :MLA_ROPE].reshape(batch, seq, MLA_ROPE),
           _block_diag_to_state(st))
    return y, new


def kernel(x_prompt, x_sample, cache_moba_k, cache_moba_v, cache_mla_latent, cache_mla_kpe, state_ret, page_table,
           g_attn, w_in, g_moba_q, g_moba_k, g_ret_out, g_mla_qlat, w_mla_q_up, g_mla_qn, g_mla_qp, g_mla_kvlat,
           w_mla_kv_up, g_mla_kn, g_mla_kp, w_out, g_mlp, w_mlp_up, w_mlp_down):
    batch, seq, d_model = x_prompt.shape
    n_req, n_dec, _ = x_sample.shape
    depth = w_in.shape[0]
    params = (g_attn, w_in, g_moba_q, g_moba_k, g_ret_out, g_mla_qlat, w_mla_q_up, g_mla_qn, g_mla_qp,
              g_mla_kvlat, w_mla_kv_up, g_mla_kn, g_mla_kp, w_out, g_mlp, w_mlp_up, w_mlp_down)
    yp = x_prompt.reshape(batch * seq, d_model)
    ys = x_sample.reshape(n_req * n_dec, d_model)
    caches = (cache_moba_k, cache_moba_v, cache_mla_latent, cache_mla_kpe)
    new_p, new_s = [], []
    for l in range(depth):
        lw = _layer_weights(l, *params)
        yp, st_p = _layer_prompt(yp, lw, batch, seq)
        ys, st_s = _layer_sample(l, ys, lw, caches, state_ret[l], page_table, n_req, n_dec)
        new_p.append(st_p)
        new_s.append(st_s)
    stacked_p = [jnp.stack(a) for a in zip(*new_p)]
    stacked_s = [jnp.stack(a) for a in zip(*new_s)]
    return (yp.reshape(batch, seq, d_model), ys.reshape(n_req, n_dec, d_model), *stacked_p, *stacked_s)
```

```python
import functools

import numpy as np
import jax
import jax.numpy as jnp
from jax import lax
from jax.experimental import pallas as pl
from jax.experimental.pallas import tpu as pltpu

HEAD_DIM = 64
MOBA_HEADS = 6
MOBA_KV_HEADS = 2
MOBA_GROUP = MOBA_HEADS // MOBA_KV_HEADS
MOBA_BLOCK = 256
MOBA_TOPK = 3
RET_HEADS = 4
RET_DK = 64
RET_DV = 64
RET_CHUNK = 128
MLA_HEADS = 6
MLA_Q_LORA = 256
MLA_KV_LORA = 128
MLA_NOPE = 64
MLA_ROPE = 32
MLA_V = 64
MLA_QK = MLA_NOPE + MLA_ROPE
ROPE_THETA = 10000.0
EPS = 1e-6

LANES = 128
VMEM_LIMIT = 56 * 1024 * 1024
F32 = jnp.float32
BF16 = jnp.bfloat16
NEG_INF = float("-inf")

N_IN_CHUNKS = 17


def _bf(x):
    return x.astype(BF16)


def _dot(a, b):
    return jnp.dot(a, b, preferred_element_type=F32)


def _dot_nt(a, b):
    return lax.dot_general(a, b, (((1,), (1,)), ((), ())), preferred_element_type=F32)


def _split_bf16(x):
    hi = _bf(x)
    lo = _bf(x - hi.astype(F32))
    return hi, lo


def _dot_nt_f32(a, b):
    ah, al = _split_bf16(a)
    bh, bl = _split_bf16(b)
    return _dot_nt(ah, bh) + (_dot_nt(ah, bl) + _dot_nt(al, bh))


def _dot_f32(a, b):
    ah, al = _split_bf16(a)
    bh, bl = _split_bf16(b)
    return _dot(ah, bh) + (_dot(ah, bl) + _dot(al, bh))


def _group_mean_sq(x, bd):
    hi, lo = _split_bf16(x * x)
    return _dot(hi, bd) + _dot(lo, bd)


def _lane(shape):
    return lax.broadcasted_iota(jnp.int32, shape, len(shape) - 1)


def _rotate_half(x, half):
    n = x.shape[-1]
    up = pltpu.roll(x, n - half, x.ndim - 1)
    down = pltpu.roll(x, half, x.ndim - 1)
    return jnp.where(_lane(x.shape) % (2 * half) < half, up, down)


def _rope(x, cos, sin_signed, half):
    return x * cos + _rotate_half(x, half) * sin_signed


def _row_rms(x):
    return x * lax.rsqrt(jnp.mean(x * x, axis=-1, keepdims=True) + EPS)


def _proj_kernel(x_ref, gattn_ref, win_ref, wq_ref, gains_ref, gqlat_ref, bd64_ref, bd32_ref,
                 cos64_ref, sin64_ref, cos32_ref, sin32_ref,
                 qm_ref, km_ref, vm_ref, qr_ref, kr_ref, vr_ref, gr_ref, ql_ref, ckv_ref, kpe_ref):
    x = x_ref[...]
    h = _row_rms(x) * gattn_ref[...]
    z = _dot(_bf(h), win_ref[...])
    bd64 = bd64_ref[...]
    bd32 = bd32_ref[...]
    cos64, sin64 = cos64_ref[...], sin64_ref[...]
    cos32, sin32 = cos32_ref[...], sin32_ref[...]
    g_mq, g_mk = gains_ref[0:1, :], gains_ref[1:2, :]
    g_kvlat, g_qn = gains_ref[2:3, :], gains_ref[3:4, :]
    g_qp, g_kp = gains_ref[4:5, :], gains_ref[5:6, :]

    def chunk(i):
        return z[:, i * LANES:(i + 1) * LANES]

    def norm_group(v, bd, g):
        return v * lax.rsqrt(_group_mean_sq(v, bd) + EPS) * g

    for i in range(3):
        qm_ref[:, i * LANES:(i + 1) * LANES] = _rope(norm_group(chunk(i), bd64, g_mq), cos64, sin64, 32)
    km_ref[...] = _rope(norm_group(chunk(3), bd64, g_mk), cos64, sin64, 32)
    vm_ref[...] = chunk(4)
    for i in range(2):
        qr_ref[:, i * LANES:(i + 1) * LANES] = _rope(chunk(5 + i), cos64, sin64, 32)
        kr_ref[:, i * LANES:(i + 1) * LANES] = _rope(chunk(7 + i), cos64, sin64, 32) * (RET_DK ** -0.5)
        vr_ref[:, i * LANES:(i + 1) * LANES] = chunk(9 + i)
        g = chunk(11 + i)
        gr_ref[:, i * LANES:(i + 1) * LANES] = g * jax.nn.sigmoid(g)
    zq = z[:, 13 * LANES:15 * LANES]
    ql = _dot(_bf(_row_rms(zq) * gqlat_ref[...]), wq_ref[...])
    for c in range(3):
        nope = ql[:, c * 256:c * 256 + LANES]
        pe = ql[:, c * 256 + LANES:(c + 1) * 256]
        ql_ref[:, c * 256:c * 256 + LANES] = norm_group(nope, bd64, g_qn)
        ql_ref[:, c * 256 + LANES:(c + 1) * 256] = _rope(norm_group(pe, bd32, g_qp), cos32, sin32, 16)
    ckv_ref[...] = _row_rms(chunk(15)) * g_kvlat
    kpe_ref[...] = _rope(norm_group(chunk(16), bd32, g_kp), cos32, sin32, 16)


def _in_proj_columns():
    cols = []
    for g in range(MOBA_GROUP):
        for hk in range(MOBA_KV_HEADS):
            h = hk * MOBA_GROUP + g
            cols += list(range(h * HEAD_DIM, (h + 1) * HEAD_DIM))
    cols += list(range(384, 2048))
    cols += list(range(2048, 2080)) * 2 + [-1] * 64
    return np.asarray(cols, np.int32)


def _q_up_columns():
    cols = []
    for c in range(MLA_HEADS // 2):
        for h in (2 * c, 2 * c + 1):
            cols += list(range(h * MLA_QK, h * MLA_QK + MLA_NOPE))
        for h in (2 * c, 2 * c + 1):
            cols += list(range(h * MLA_QK + MLA_NOPE, (h + 1) * MLA_QK))
        cols += [-1] * 64
    return np.asarray(cols, np.int32)


def _take_cols(w, cols):
    out = jnp.take(w, jnp.asarray(np.maximum(cols, 0)), axis=1)
    return out * jnp.asarray((cols >= 0).astype(np.float32))[None, :]


def _block_diag_avg(group):
    i = np.arange(LANES)
    return jnp.asarray(((i[:, None] // group) == (i[None, :] // group)).astype(np.float32) / group, BF16)


def _rope_tables(pos, dim):
    half = dim // 2
    inv = ROPE_THETA ** (-jnp.arange(half, dtype=F32) / half)
    ang = pos.astype(F32)[:, None] * inv[None, :]
    cos, sin = jnp.cos(ang), jnp.sin(ang)
    reps = LANES // dim
    return jnp.tile(jnp.concatenate([cos, cos], -1), (1, reps)), jnp.tile(jnp.concatenate([-sin, sin], -1), (1, reps))


def _tile_rows(n_rows, cap):
    t = min(cap, n_rows)
    assert n_rows % t == 0 and t % 8 == 0, (n_rows, t)
    return t


def _project(x2d, pos, n_pos_tiles, lw, tile):
    n, d = x2d.shape
    c64, s64 = _rope_tables(pos, 64)
    c32, s32 = _rope_tables(pos, 32)
    grid = (n // tile,)
    row = lambda w: pl.BlockSpec((tile, w), lambda i: (i, 0))
    const = lambda a: pl.BlockSpec(a.shape, lambda i: (0,) * a.ndim)
    tab = pl.BlockSpec((tile, LANES), lambda i: (i % n_pos_tiles, 0))
    widths = (384, 128, 128, 256, 256, 256, 256, 768, 128, 128)
    consts = (lw["g_attn"], lw["w_in"], lw["w_q_up"], lw["gains"], lw["g_qlat"], lw["bd64"], lw["bd32"])
    return pl.pallas_call(
        _proj_kernel,
        grid=grid,
        in_specs=[row(d)] + [const(a) for a in consts] + [tab] * 4,
        out_specs=[row(w) for w in widths],
        out_shape=[jax.ShapeDtypeStruct((n, w), F32) for w in widths],
        compiler_params=pltpu.CompilerParams(dimension_semantics=("arbitrary",), vmem_limit_bytes=VMEM_LIMIT),
        name="in_proj",
    )(x2d, *consts, c64, s64, c32, s32)


def _topk_mask(gate, n_valid, k):
    col = _lane(gate.shape)
    g = jnp.where(col < n_valid, gate, NEG_INF)
    sel = jnp.zeros(gate.shape, F32)
    big = jnp.int32(gate.shape[-1])
    for _ in range(k):
        m = jnp.max(g, axis=-1, keepdims=True)
        idx = jnp.min(jnp.where(g == m, col, big), axis=-1, keepdims=True)
        pick = (col == idx) & (m > NEG_INF)
        sel = jnp.where(pick, 1.0, sel)
        g = jnp.where(pick, NEG_INF, g)
    return sel


def _topk_mask_t(gate, n_valid, k):
    row = lax.broadcasted_iota(jnp.int32, gate.shape, 0)
    g = jnp.where(row < n_valid, gate, NEG_INF)
    sel = jnp.zeros(gate.shape, F32)
    big = jnp.int32(gate.shape[0])
    for _ in range(k):
        m = jnp.max(g, axis=0, keepdims=True)
        idx = jnp.min(jnp.where(g == m, row, big), axis=0, keepdims=True)
        pick = (row == idx) & (m > NEG_INF)
        sel = jnp.where(pick, 1.0, sel)
        g = jnp.where(pick, NEG_INF, g)
    return sel


def _moba_prompt_kernel(q_ref, k_ref, v_ref, o_ref, kb_sc, vt_sc, kmean_sc, qf_sc, qs_sc, sel_sc, m_sc, l_sc, acc_sc):
    tq = q_ref.shape[0]
    n_blk = kmean_sc.shape[0]
    qi = pl.program_id(1)

    @pl.when(qi == 0)
    def _():
        kb_sc[...] = _bf(k_ref[...])
        for j in range(n_blk):
            blk = slice(j * MOBA_BLOCK, (j + 1) * MOBA_BLOCK)
            vt_sc[j] = _bf(v_ref[blk, :].T)
            kmean_sc[j:j + 1, :] = jnp.mean(k_ref[blk, :], axis=0, keepdims=True)

    dim_row = lax.broadcasted_iota(jnp.int32, (LANES, tq), 0)
    for g in range(MOBA_GROUP):
        chunk_t = q_ref[:, g * LANES:(g + 1) * LANES].T
        for hk in range(MOBA_KV_HEADS):
            s = g * MOBA_KV_HEADS + hk
            qf_sc[:, s * tq:(s + 1) * tq] = jnp.where((dim_row // HEAD_DIM) == hk, chunk_t, 0.0)
    q_t = qf_sc[...]
    sel_sc[...] = _topk_mask_t(_dot_f32(kmean_sc[...], q_t), qi, MOBA_TOPK)
    qs_sc[...] = _bf(q_t * (HEAD_DIM ** -0.5))

    own = pl.multiple_of(qi * MOBA_BLOCK, MOBA_BLOCK)
    s = _dot(kb_sc[pl.ds(own, MOBA_BLOCK), :], qs_sc[...])
    kpos = lax.broadcasted_iota(jnp.int32, s.shape, 0)
    s = jnp.where(kpos <= _lane(s.shape) % tq, s, NEG_INF)
    m = jnp.max(s, axis=0, keepdims=True)
    p = jnp.exp(s - m)
    m_sc[...] = m
    l_sc[...] = jnp.sum(p, axis=0, keepdims=True)
    acc_sc[...] = _dot(vt_sc[qi], _bf(p))

    def past(j, carry):
        start = pl.multiple_of(j * MOBA_BLOCK, MOBA_BLOCK)
        chosen = sel_sc[pl.ds(j, 1), :] > 0.5
        s = jnp.where(chosen, _dot(kb_sc[pl.ds(start, MOBA_BLOCK), :], qs_sc[...]), NEG_INF)
        m_old = m_sc[...]
        m_new = jnp.maximum(m_old, jnp.max(s, axis=0, keepdims=True))
        alpha = jnp.exp(m_old - m_new)
        p = jnp.exp(s - m_new)
        m_sc[...] = m_new
        l_sc[...] = alpha * l_sc[...] + jnp.sum(p, axis=0, keepdims=True)
        acc_sc[...] = alpha * acc_sc[...] + _dot(vt_sc[j], _bf(p))
        return carry

    lax.fori_loop(0, qi, past, 0)

    o_t = acc_sc[...] / l_sc[...]
    for g in range(MOBA_GROUP):
        c0, c1 = (2 * g) * tq, (2 * g + 1) * tq
        pair_t = jnp.where(dim_row < HEAD_DIM, o_t[:, c0:c0 + tq], o_t[:, c1:c1 + tq])
        o_ref[:, g * LANES:(g + 1) * LANES] = pair_t.T


def _moba_prompt(q_m, k_m, v_m, batch, seq):
    assert seq % MOBA_BLOCK == 0
    tq = MOBA_BLOCK
    n_blk = seq // MOBA_BLOCK
    nq = seq // tq
    rows = MOBA_HEADS * tq
    return pl.pallas_call(
        _moba_prompt_kernel,
        grid=(batch, nq),
        in_specs=[pl.BlockSpec((tq, 384), lambda b, i: (b * nq + i, 0)),
                  pl.BlockSpec((seq, LANES), lambda b, i: (b, 0)),
                  pl.BlockSpec((seq, LANES), lambda b, i: (b, 0))],
        out_specs=pl.BlockSpec((tq, 384), lambda b, i: (b * nq + i, 0)),
        out_shape=jax.ShapeDtypeStruct((batch * seq, 384), F32),
        scratch_shapes=[pltpu.VMEM((seq, LANES), BF16), pltpu.VMEM((n_blk, LANES, MOBA_BLOCK), BF16),
                        pltpu.VMEM((n_blk, LANES), F32), pltpu.VMEM((LANES, rows), F32),
                        pltpu.VMEM((LANES, rows), BF16), pltpu.VMEM((n_blk, rows), F32),
                        pltpu.VMEM((1, rows), F32), pltpu.VMEM((1, rows), F32), pltpu.VMEM((LANES, rows), F32)],
        compiler_params=pltpu.CompilerParams(dimension_semantics=("arbitrary", "arbitrary"),
                                             vmem_limit_bytes=VMEM_LIMIT),
        name="moba_prompt",
    )(q_m, k_m, v_m)


def _mla_prompt_kernel(q_ref, ckv_ref, kpe_ref, wkv_ref, gkn_ref, bd64_ref, o_ref,
                       kcat_sc, vt_sc, qs_sc, m_sc, l_sc, acc_sc):
    tq = q_ref.shape[0]
    seq = ckv_ref.shape[0]
    n_pairs = MLA_HEADS // 2
    qi = pl.program_id(1)

    @pl.when(qi == 0)
    def _():
        def expand(t, carry):
            r = pl.ds(pl.multiple_of(t * tq, tq), tq)
            kv = _dot(_bf(ckv_ref[r, :]), wkv_ref[...])
            pe = _bf(kpe_ref[r, :])
            for c in range(n_pairs):
                kn = kv[:, c * LANES:(c + 1) * LANES]
                kn = kn * lax.rsqrt(_group_mean_sq(kn, bd64_ref[...]) + EPS) * gkn_ref[...]
                kcat_sc[c, r, 0:LANES] = _bf(kn)
                kcat_sc[c, r, LANES:2 * LANES] = pe
                vt_sc[c, t] = _bf(kv[:, (n_pairs + c) * LANES:(n_pairs + c + 1) * LANES].T)
            return carry
        lax.fori_loop(0, seq // tq, expand, 0)

    depth = lax.broadcasted_iota(jnp.int32, (256, tq), 0)
    for c in range(n_pairs):
        chunk_t = (q_ref[:, c * 256:(c + 1) * 256] * (MLA_QK ** -0.5)).T
        for hh in range(2):
            nope = (depth < LANES) & ((depth // MLA_NOPE) == hh)
            pe = (depth >= LANES) & (depth < LANES + 2 * MLA_ROPE) & (((depth - LANES) // MLA_ROPE) == hh)
            qs_sc[c, :, hh * tq:(hh + 1) * tq] = _bf(jnp.where(nope | pe, chunk_t, 0.0))

    own = pl.multiple_of(qi * tq, tq)
    for c in range(n_pairs):
        s = _dot(kcat_sc[c, pl.ds(own, tq), :], qs_sc[c])
        kpos = lax.broadcasted_iota(jnp.int32, s.shape, 0)
        s = jnp.where(kpos <= _lane(s.shape) % tq, s, NEG_INF)
        m = jnp.max(s, axis=0, keepdims=True)
        p = jnp.exp(s - m)
        m_sc[c] = m
        l_sc[c] = jnp.sum(p, axis=0, keepdims=True)
        acc_sc[c] = _dot(vt_sc[c, qi], _bf(p))

    def past(j, carry):
        start = pl.multiple_of(j * tq, tq)
        for c in range(n_pairs):
            s = _dot(kcat_sc[c, pl.ds(start, tq), :], qs_sc[c])
            m_old = m_sc[c]
            m_new = jnp.maximum(m_old, jnp.max(s, axis=0, keepdims=True))
            alpha = jnp.exp(m_old - m_new)
            p = jnp.exp(s - m_new)
            m_sc[c] = m_new
            l_sc[c] = alpha * l_sc[c] + jnp.sum(p, axis=0, keepdims=True)
            acc_sc[c] = alpha * acc_sc[c] + _dot(vt_sc[c, j], _bf(p))
        return carry

    lax.fori_loop(0, qi, past, 0)

    dim_row = lax.broadcasted_iota(jnp.int32, (LANES, tq), 0)
    for c in range(n_pairs):
        o_t = acc_sc[c] / l_sc[c]
        o_ref[:, c * LANES:(c + 1) * LANES] = jnp.where(dim_row < MLA_V, o_t[:, 0:tq], o_t[:, tq:2 * tq]).T


def _mla_prompt(q_l, c_kv, kpe_slot, lw, batch, seq):
    tq = 256
    assert seq % tq == 0
    nq = seq // tq
    n_pairs = MLA_HEADS // 2
    const = lambda a: pl.BlockSpec(a.shape, lambda b, i: (0,) * a.ndim)
    return pl.pallas_call(
        _mla_prompt_kernel,
        grid=(batch, nq),
        in_specs=[pl.BlockSpec((tq, 768), lambda b, i: (b * nq + i, 0)),
                  pl.BlockSpec((seq, LANES), lambda b, i: (b, 0)),
                  pl.BlockSpec((seq, LANES), lambda b, i: (b, 0)),
                  const(lw["w_kv_up"]), const(lw["g_kn"]), const(lw["bd64"])],
        out_specs=pl.BlockSpec((tq, 384), lambda b, i: (b * nq + i, 0)),
        out_shape=jax.ShapeDtypeStruct((batch * seq, 384), F32),
        scratch_shapes=[pltpu.VMEM((n_pairs, seq, 256), BF16), pltpu.VMEM((n_pairs, nq, LANES, tq), BF16),
                        pltpu.VMEM((n_pairs, 256, 2 * tq), BF16), pltpu.VMEM((n_pairs, 1, 2 * tq), F32),
                        pltpu.VMEM((n_pairs, 1, 2 * tq), F32), pltpu.VMEM((n_pairs, LANES, 2 * tq), F32)],
        compiler_params=pltpu.CompilerParams(dimension_semantics=("arbitrary", "arbitrary"),
                                             vmem_limit_bytes=VMEM_LIMIT),
        name="mla_prompt",
    )(q_l, c_kv, kpe_slot, lw["w_kv_up"], lw["g_kn"], lw["bd64"])


def _retention_kernel(q_ref, k_ref, v_ref, gate_ref, st_ref, dec_ref, gpow_ref, wts_ref, gc_ref, bdm_ref,
                      gout_ref, bd64_ref, o_ref, sto_ref, st_sc):
    chunk = dec_ref.shape[-1]
    n_seqs, n_pairs = st_ref.shape[0], st_ref.shape[1]
    seq = q_ref.shape[0] // n_seqs
    lane = _lane((chunk, LANES))

    for i in range(n_seqs):
        for p in range(n_pairs):
            st_sc[p] = st_ref[i, p]

        def step(t, carry):
            r = pl.ds(pl.multiple_of(i * seq + t * chunk, chunk), chunk)
            for p in range(n_pairs):
                cols = slice(p * LANES, (p + 1) * LANES)
                q, k, v = q_ref[r, cols], k_ref[r, cols], v_ref[r, cols]
                kb, vb = _bf(k), _bf(v)
                halves = []
                for hh in range(2):
                    qh = _bf(jnp.where((lane // RET_DK) == hh, q, 0.0))
                    s = _dot_nt(qh, kb) * dec_ref[p, hh]
                    halves.append(_dot(_bf(s), vb))
                o = jnp.where(lane < RET_DV, halves[0], halves[1])
                o = o + _dot(_bf(q), _bf(st_sc[p])) * gpow_ref[p]
                u = _dot(_bf((k * wts_ref[p]).T), vb)
                st_sc[p] = st_sc[p] * gc_ref[p] + u * bdm_ref[...]
                o = o * lax.rsqrt(_group_mean_sq(o, bd64_ref[...]) + EPS) * gout_ref[...]
                o_ref[r, cols] = o * gate_ref[r, cols]
            return carry

        lax.fori_loop(0, seq // chunk, step, 0)
        for p in range(n_pairs):
            sto_ref[i, p] = st_sc[p]


def _retention_tables(chunk):
    lg = jnp.log(1.0 - 2.0 ** (-5.0 - jnp.arange(RET_HEADS, dtype=F32)))
    idx = jnp.arange(chunk, dtype=F32)
    diff = idx[:, None] - idx[None, :]
    decay = jnp.where(diff >= 0, jnp.exp(lg[:, None, None] * jnp.maximum(diff, 0.0)), 0.0)
    gpow = jnp.exp(lg[None, :] * (idx[:, None] + 1.0))
    wts = jnp.exp(lg[:, None] * (chunk - 1.0 - idx)[None, :]).T
    gc = jnp.exp(lg * chunk)
    n_pairs = RET_HEADS // 2
    lanes = lambda t: jnp.repeat(t.reshape(chunk, n_pairs, 2), RET_DV, axis=2).transpose(1, 0, 2)
    i = np.arange(LANES)
    bdm = jnp.asarray(((i[:, None] // RET_DK) == (i[None, :] // RET_DV)).astype(np.float32))
    gc_l = jnp.repeat(gc.reshape(n_pairs, 2), RET_DK, axis=1)
    gc_t = gc_l[:, :, None] * bdm[None]
    return decay.reshape(n_pairs, 2, chunk, chunk), lanes(gpow), lanes(wts), gc_t, bdm


def _retention(q_r, k_r, v_r, gate_r, state_bd, lw, batch, seq):
    chunk = min(RET_CHUNK, seq)
    assert seq % chunk == 0
    n_pairs = RET_HEADS // 2
    dec, gpow, wts, gc_t, bdm = _retention_tables(chunk)
    per_step = 8 if (seq == chunk and batch % 8 == 0) else 1
    act = pl.BlockSpec((per_step * seq, n_pairs * LANES), lambda b: (b, 0))
    st = pl.BlockSpec((per_step, n_pairs, LANES, LANES), lambda b: (b, 0, 0, 0))
    const = lambda a: pl.BlockSpec(a.shape, lambda b: (0,) * a.ndim)
    return pl.pallas_call(
        _retention_kernel,
        grid=(batch // per_step,),
        in_specs=[act, act, act, act, st, const(dec), const(gpow), const(wts), const(gc_t),
                  const(bdm), const(lw["g_ret_out"]), const(lw["bd64"])],
        out_specs=[act, st],
        out_shape=[jax.ShapeDtypeStruct((batch * seq, n_pairs * LANES), F32),
                   jax.ShapeDtypeStruct((batch, n_pairs, LANES, LANES), F32)],
        scratch_shapes=[pltpu.VMEM((n_pairs, LANES, LANES), F32)],
        compiler_params=pltpu.CompilerParams(dimension_semantics=("arbitrary",),
                                             vmem_limit_bytes=VMEM_LIMIT),
        name="retention",
    )(q_r, k_r, v_r, gate_r, state_bd, dec, gpow, wts, gc_t, bdm, lw["g_ret_out"], lw["bd64"])


def _state_to_block_diag(st):
    b = st.shape[0]
    s = st.reshape(b, RET_HEADS // 2, 2, RET_DK, RET_DV)
    eye = jnp.eye(2, dtype=st.dtype)
    return jnp.einsum("bpird,ij->bpirjd", s, eye).reshape(b, RET_HEADS // 2, 2 * RET_DK, 2 * RET_DV)


def _block_diag_to_state(bd):
    b = bd.shape[0]
    s = bd.reshape(b, RET_HEADS // 2, 2, RET_DK, 2, RET_DV)
    return jnp.stack([s[:, :, 0, :, 0, :], s[:, :, 1, :, 1, :]], axis=2).reshape(b, RET_HEADS, RET_DK, RET_DV)


def _out_mlp_kernel(x_ref, om_ref, or_ref, ol_ref, wom_ref, wor_ref, wol_ref, gmlp_ref, wup_ref, wdn_ref, y_ref):
    x1 = x_ref[...] + (_dot(_bf(om_ref[...]), wom_ref[...]) + _dot(_bf(or_ref[...]), wor_ref[...])
                       + _dot(_bf(ol_ref[...]), wol_ref[...]))
    h = _bf(_row_rms(x1) * gmlp_ref[...])
    d_ff = wup_ref.shape[1]
    step = min(1024, d_ff)
    acc = x1
    for c in range(d_ff // step):
        u = jnp.maximum(_dot(h, wup_ref[:, c * step:(c + 1) * step]), 0.0)
        acc = acc + _dot(_bf(u * u), wdn_ref[c * step:(c + 1) * step, :])
    y_ref[...] = acc


def _out_mlp(x2d, o_m, o_r, o_l, lw, tile):
    n, d = x2d.shape
    row = lambda w: pl.BlockSpec((tile, w), lambda i: (i, 0))
    const = lambda a: pl.BlockSpec(a.shape, lambda i: (0,) * a.ndim, pipeline_mode=pl.Buffered(1))
    consts = (lw["w_out_m"], lw["w_out_r"], lw["w_out_l"], lw["g_mlp"], lw["w_mlp_up"], lw["w_mlp_down"])
    return pl.pallas_call(
        _out_mlp_kernel,
        grid=(n // tile,),
        in_specs=[row(d), row(384), row(256), row(384)] + [const(a) for a in consts],
        out_specs=row(d),
        out_shape=jax.ShapeDtypeStruct((n, d), F32),
        compiler_params=pltpu.CompilerParams(dimension_semantics=("arbitrary",), vmem_limit_bytes=VMEM_LIMIT),
        name="out_mlp",
    )(x2d, o_m, o_r, o_l, *consts)


def _layer_weights(l, g_attn, w_in, g_moba_q, g_moba_k, g_ret_out, g_mla_qlat, w_mla_q_up, g_mla_qn, g_mla_qp,
                   g_mla_kvlat, w_mla_kv_up, g_mla_kn, g_mla_kp, w_out, g_mlp, w_mlp_up, w_mlp_down):
    ones64 = jnp.ones((64,), F32)
    gains = jnp.stack([
        jnp.tile(g_moba_q[l], 2), jnp.tile(g_moba_k[l], 2), g_mla_kvlat[l], jnp.tile(g_mla_qn[l], 2),
        jnp.concatenate([g_mla_qp[l], g_mla_qp[l], ones64]), jnp.concatenate([g_mla_kp[l], g_mla_kp[l], ones64]),
        jnp.ones((LANES,), F32), jnp.ones((LANES,), F32)])
    kv = w_mla_kv_up[l].reshape(MLA_KV_LORA, MLA_HEADS, MLA_NOPE + MLA_V)
    w_kn = kv[:, :, :MLA_NOPE].reshape(MLA_KV_LORA, MLA_HEADS * MLA_NOPE)
    w_v = kv[:, :, MLA_NOPE:].reshape(MLA_KV_LORA, MLA_HEADS * MLA_V)
    wo = w_out[l]
    perm = [(hk * MOBA_GROUP + g) for g in range(MOBA_GROUP) for hk in range(MOBA_KV_HEADS)]
    wo_m = wo[:MOBA_HEADS * HEAD_DIM].reshape(MOBA_HEADS, HEAD_DIM, -1)[jnp.asarray(perm)].reshape(MOBA_HEADS * HEAD_DIM, -1)
    r0 = MOBA_HEADS * HEAD_DIM
    r1 = r0 + RET_HEADS * RET_DV
    return {
        "g_attn": g_attn[l][None, :],
        "w_in": _bf(_take_cols(w_in[l], _in_proj_columns())),
        "w_q_up": _bf(_take_cols(w_mla_q_up[l], _q_up_columns())),
        "gains": gains,
        "g_qlat": g_mla_qlat[l][None, :],
        "bd64": _block_diag_avg(64),
        "bd32": _block_diag_avg(32),
        "w_kv_up": _bf(jnp.concatenate([w_kn, w_v], axis=1)),
        "w_kn_f32": w_kn,
        "w_v": _bf(w_v),
        "g_kn": jnp.tile(g_mla_kn[l], 2)[None, :],
        "g_ret_out": jnp.tile(g_ret_out[l], 2)[None, :],
        "w_out_m": _bf(wo_m), "w_out_r": _bf(wo[r0:r1]), "w_out_l": _bf(wo[r1:]),
        "g_mlp": g_mlp[l][None, :],
        "w_mlp_up": _bf(w_mlp_up[l]), "w_mlp_down": _bf(w_mlp_down[l]),
    }


def _rows_of_page(page):
    return lambda buf, slot, pg: buf.at[slot, pl.ds(pg * page, page), :]


def _whole_page(buf, slot, pg):
    return buf.at[slot, pg]


def _page_copies(pt_ref, layer, req, chunk, slot, pages_per_chunk, streams):
    copies = []
    for hbm, buf, sem, window in streams:
        for pg in range(pages_per_chunk):
            pid = pt_ref[req, chunk * pages_per_chunk + pg]
            copies.append(pltpu.make_async_copy(hbm.at[layer, pid], window(buf, slot, pg), sem.at[slot]))
    return copies


MLA_GATHER_SLOTS = 4
MOBA_GATHER_SLOTS = 8


def _gather_slots(streams):
    slots = streams[0][1].shape[0]
    assert all(buf.shape[0] == slots for _, buf, _, _ in streams)
    return slots


def _gather_first(pt_ref, layer, n_ch, pages_per_chunk, streams):
    n_ahead = _gather_slots(streams) - 1
    assert n_ch >= n_ahead

    @pl.when(pl.program_id(0) == 0)
    def _():
        for s in range(n_ahead):
            for c in _page_copies(pt_ref, layer, 0, s, s, pages_per_chunk, streams):
                c.start()


def _gather_chunk(pt_ref, layer, ch, n_ch, pages_per_chunk, streams):
    slots = _gather_slots(streams)
    r, n_req = pl.program_id(0), pl.num_programs(0)
    step = r * n_ch + ch
    slot = step % slots
    ahead = ch + (slots - 1)
    wrap = ahead >= n_ch
    nr = jnp.where(wrap, jnp.minimum(r + 1, n_req - 1), r)
    nc = jnp.where(wrap, ahead - n_ch, ahead)
    for c in _page_copies(pt_ref, layer, nr, nc, (step + slots - 1) % slots, pages_per_chunk, streams):
        c.start()
    for c in _page_copies(pt_ref, layer, r, ch, slot, pages_per_chunk, streams):
        c.wait()
    return slot


def _gather_drain(pt_ref, layer, n_ch, pages_per_chunk, streams):
    slots = _gather_slots(streams)
    r, n_req = pl.program_id(0), pl.num_programs(0)

    @pl.when(r == n_req - 1)
    def _():
        for i in range(slots - 1):
            slot = (n_req * n_ch + i) % slots
            for c in _page_copies(pt_ref, layer, r, i, slot, pages_per_chunk, streams):
                c.wait()


def _mla_decode_kernel(layer, n_ch, pages_per_chunk, page, sub,
                       pt_ref, lat_hbm, kpe_hbm, qn_ref, qpe_ref, cnew_ref, penew_ref, wkt_ref, gkn_ref,
                       wv_ref, o_ref, cbuf, pbuf, sem_c, sem_p, lhs_sc, cb_sc, s_sc, m_sc, l_sc, acc_sc):
    n_dec = cnew_ref.shape[0]
    rows = MLA_HEADS * n_dec
    n_y = MLA_HEADS * MLA_NOPE
    streams = ((lat_hbm, cbuf, sem_c, _rows_of_page(page)), (kpe_hbm, pbuf, sem_p, _whole_page))
    _gather_first(pt_ref, layer, n_ch, pages_per_chunk, streams)

    lhs_sc[0:n_y, :] = wkt_ref[...]
    for h in range(MLA_HEADS):
        qn = qn_ref[0, h * n_dec:(h + 1) * n_dec, :] * gkn_ref[...]
        lhs_sc[n_y + h * n_dec:n_y + (h + 1) * n_dec, :] = _dot(
            _bf(qn), _bf(wkt_ref[h * MLA_NOPE:(h + 1) * MLA_NOPE, :]))
    m_sc[...] = jnp.full(m_sc.shape, NEG_INF, F32)
    l_sc[...] = jnp.zeros(l_sc.shape, F32)
    acc_sc[...] = jnp.zeros(acc_sc.shape, F32)

    lhs = _bf(lhs_sc[...])
    qpe = _bf(qpe_ref[0])
    scale = MLA_QK ** -0.5

    def scores(cb, s_pe):
        r = _dot_nt(lhs, cb)
        per_head = []
        for h in range(MLA_HEADS):
            y = r[h * MLA_NOPE:(h + 1) * MLA_NOPE, :]
            rinv = lax.rsqrt(jnp.mean(y * y, axis=0, keepdims=True) + EPS)
            per_head.append(r[n_y + h * n_dec:n_y + (h + 1) * n_dec, :] * rinv)
        return (jnp.concatenate(per_head, axis=0) + s_pe) * scale

    def update(s, cb):
        m_old = m_sc[...]
        m_new = jnp.maximum(m_old, jnp.max(s, axis=-1, keepdims=True))
        alpha = jnp.exp(m_old - m_new)
        p = jnp.exp(s - m_new)
        m_sc[...] = m_new
        l_sc[...] = alpha * l_sc[...] + jnp.sum(p, axis=-1, keepdims=True)
        acc_sc[...] = alpha * acc_sc[...] + _dot(_bf(p), cb)

    def chunk(ch, carry):
        slot = _gather_chunk(pt_ref, layer, ch, n_ch, pages_per_chunk, streams)
        s_pe = _dot(qpe, _bf(jnp.concatenate([pbuf[slot, pg] for pg in range(pages_per_chunk)], axis=1)))
        for t in range((pages_per_chunk * page) // sub):
            rws = slice(t * sub, (t + 1) * sub)
            cb = _bf(cbuf[slot, rws, :])
            cb_sc[rws, :] = cb
            s_sc[:, rws] = scores(cb, s_pe[:, rws])
        update(s_sc[...], cb_sc[...])
        return carry

    lax.fori_loop(0, n_ch, chunk, 0)

    cb = _bf(cnew_ref[...])
    s = scores(cb, _dot_nt(qpe, _bf(penew_ref[0])))
    t_row = lax.broadcasted_iota(jnp.int32, s.shape, 0) % n_dec
    update(jnp.where(_lane(s.shape) <= t_row, s, NEG_INF), cb)
    o_lat = _bf(acc_sc[...] / l_sc[...])
    lane = _lane((n_dec, LANES))
    for c in range(MLA_HEADS // 2):
        wv = wv_ref[:, c * LANES:(c + 1) * LANES]
        o0 = _dot(o_lat[(2 * c) * n_dec:(2 * c + 1) * n_dec, :], wv)
        o1 = _dot(o_lat[(2 * c + 1) * n_dec:(2 * c + 2) * n_dec, :], wv)
        o_ref[:, c * LANES:(c + 1) * LANES] = jnp.where(lane < MLA_V, o0, o1)
    _gather_drain(pt_ref, layer, n_ch, pages_per_chunk, streams)


def _decode_chunk_keys(past_len, page):
    keys = min(2048, max(MOBA_BLOCK, past_len // 8))
    assert past_len % keys == 0 and keys % page == 0, (past_len, keys, page)
    return keys


def _mla_decode(layer, page_table, lat_cache, kpe_cache, q_l, c_new, kpe_slot, lw, n_req, n_dec):
    page = lat_cache.shape[2]
    past_len = page_table.shape[1] * page
    keys = _decode_chunk_keys(past_len, page)
    sub = min(256, keys)
    rows = MLA_HEADS * n_dec
    arr = q_l.reshape(n_req, n_dec, MLA_HEADS // 2, 256)
    heads_first = lambda a, w: a.reshape(n_req, n_dec, MLA_HEADS // 2, 2, w).transpose(0, 2, 3, 1, 4).reshape(n_req, rows, w)
    qn_rows = heads_first(arr[..., :LANES], MLA_NOPE)
    qpe_rows = heads_first(arr[..., LANES:LANES + 2 * MLA_ROPE], MLA_ROPE)
    pe_new = kpe_slot[:, :MLA_ROPE].reshape(n_req, n_dec, MLA_ROPE)
    assert sub % page == 0 and page == LANES
    w_kn_t = lw["w_kn_f32"].T
    g_kn = lw["g_kn"][:, :MLA_NOPE]
    const = lambda a: pl.BlockSpec(a.shape, lambda r, pt: (0,) * a.ndim)
    kern = functools.partial(_mla_decode_kernel, layer, past_len // keys, keys // page, page, sub)
    return pl.pallas_call(
        kern,
        grid_spec=pltpu.PrefetchScalarGridSpec(
            num_scalar_prefetch=1,
            grid=(n_req,),
            in_specs=[pl.BlockSpec(memory_space=pl.ANY), pl.BlockSpec(memory_space=pl.ANY),
                      pl.BlockSpec((1, rows, MLA_NOPE), lambda r, pt: (r, 0, 0)),
                      pl.BlockSpec((1, rows, MLA_ROPE), lambda r, pt: (r, 0, 0)),
                      pl.BlockSpec((n_dec, LANES), lambda r, pt: (r, 0)),
                      pl.BlockSpec((1, n_dec, MLA_ROPE), lambda r, pt: (r, 0, 0)),
                      const(w_kn_t), const(g_kn), const(lw["w_v"])],
            out_specs=pl.BlockSpec((n_dec, MLA_HEADS * MLA_V), lambda r, pt: (r, 0)),
            scratch_shapes=[pltpu.VMEM((MLA_GATHER_SLOTS, keys, LANES), F32),
                            pltpu.VMEM((MLA_GATHER_SLOTS, keys // page, MLA_ROPE, page), F32),
                            pltpu.SemaphoreType.DMA((MLA_GATHER_SLOTS,)), pltpu.SemaphoreType.DMA((MLA_GATHER_SLOTS,)),
                            pltpu.VMEM((MLA_HEADS * MLA_NOPE + rows, LANES), F32),
                            pltpu.VMEM((keys, LANES), BF16), pltpu.VMEM((rows, keys), F32),
                            pltpu.VMEM((rows, 1), F32), pltpu.VMEM((rows, 1), F32), pltpu.VMEM((rows, LANES), F32)]),
        out_shape=jax.ShapeDtypeStruct((n_req * n_dec, MLA_HEADS * MLA_V), F32),
        compiler_params=pltpu.CompilerParams(dimension_semantics=("arbitrary",),
                                             vmem_limit_bytes=VMEM_LIMIT),
        name="mla_decode",
    )(page_table, lat_cache, kpe_cache.transpose(0, 1, 3, 2), qn_rows, qpe_rows, c_new, pe_new, w_kn_t, g_kn,
      lw["w_v"])


def _moba_decode_kernel(layer, n_ch, pages_per_chunk, page,
                        pt_ref, k_hbm, v_hbm, q_ref, knew_ref, vnew_ref, expand_ref, o_ref,
                        kbuf, vbuf, sem_k, sem_v, vall_sc, s_sc):
    n_dec = knew_ref.shape[0]
    keys = pages_per_chunk * page
    blocks_per_chunk = keys // MOBA_BLOCK
    n_blk = n_ch * blocks_per_chunk
    pages_per_block = MOBA_BLOCK // page
    streams = ((k_hbm, kbuf, sem_k, _whole_page), (v_hbm, vbuf, sem_v, _whole_page))
    _gather_first(pt_ref, layer, n_ch, pages_per_chunk, streams)
    q = q_ref[0]
    qb = _bf(q * (HEAD_DIM ** -0.5))
    rows = q.shape[0]
    blk_col = _lane((page, n_blk))

    def chunk(ch, carry):
        kmean, bmax = carry
        slot = _gather_chunk(pt_ref, layer, ch, n_ch, pages_per_chunk, streams)
        k_pages = [kbuf[slot, pg].reshape(LANES, page) for pg in range(pages_per_chunk)]
        s = _dot(qb, _bf(jnp.concatenate(k_pages, axis=1)))
        s_sc[ch] = s
        for b in range(blocks_per_chunk):
            best = s[:, b * MOBA_BLOCK:b * MOBA_BLOCK + page]
            for i in range(1, pages_per_block):
                best = jnp.maximum(best, s[:, b * MOBA_BLOCK + i * page:b * MOBA_BLOCK + (i + 1) * page])
            bmax = jnp.where(_lane(bmax.shape) == ch * blocks_per_chunk + b,
                             jnp.max(best, axis=-1, keepdims=True), bmax)
        for pg in range(pages_per_chunk):
            vall_sc[ch, :, pg * page:(pg + 1) * page] = _bf(vbuf[slot, pg].reshape(LANES, page))
        for b in range(blocks_per_chunk):
            ksum = k_pages[b * pages_per_block]
            for i in range(1, pages_per_block):
                ksum = ksum + k_pages[b * pages_per_block + i]
            hi, lo = _split_bf16(ksum)
            pick = _bf(jnp.where(blk_col == ch * blocks_per_chunk + b, 1.0 / MOBA_BLOCK, 0.0))
            kmean = kmean + (_dot(hi, pick) + _dot(lo, pick))
        return kmean, bmax

    kmean, bmax = lax.fori_loop(0, n_ch, chunk,
                                (jnp.zeros((LANES, n_blk), F32), jnp.full((rows, n_blk), NEG_INF, F32)))

    sel_f = _topk_mask(_dot_f32(q, kmean), n_blk, MOBA_TOPK)
    sel = _bf(sel_f)
    s_own = _dot_nt(qb, _bf(knew_ref[...]))
    t_row = lax.broadcasted_iota(jnp.int32, s_own.shape, 0) % n_dec
    s_own = jnp.where(_lane(s_own.shape) <= t_row, s_own, NEG_INF)
    def chunk_mask(c):
        return _dot(sel, expand_ref[c]) > 0.5

    def fold(x, op):
        out = x[:, 0:page]
        for i in range(1, keys // page):
            out = op(out, x[:, i * page:(i + 1) * page])
        return out

    m_sel = jnp.max(jnp.where(sel_f > 0.5, bmax, NEG_INF), axis=-1, keepdims=True)
    m = jnp.maximum(m_sel, jnp.max(s_own, axis=-1, keepdims=True))

    def chunk_pv(c, carry):
        lsum, acc = carry
        p = jnp.where(chunk_mask(c), jnp.exp(s_sc[c] - m), 0.0)
        return lsum + fold(p, jnp.add), acc + _dot_nt(_bf(p), vall_sc[c])

    p_own = jnp.exp(s_own - m)
    lsum, acc = lax.fori_loop(0, n_ch, chunk_pv,
                              (jnp.zeros((rows, page), F32), _dot(_bf(p_own), _bf(vnew_ref[...]))),
                              unroll=min(4, n_ch))
    l = jnp.sum(lsum, axis=-1, keepdims=True) + jnp.sum(p_own, axis=-1, keepdims=True)
    o_ref[0] = acc / l
    _gather_drain(pt_ref, layer, n_ch, pages_per_chunk, streams)


def _moba_decode(layer, page_table, k_cache, v_cache, q_m, k_new, v_new, n_req, n_dec):
    page = k_cache.shape[2]
    past_len = page_table.shape[1] * page
    assert past_len % MOBA_BLOCK == 0 and MOBA_BLOCK % page == 0
    keys = _decode_chunk_keys(past_len, page)
    assert keys % MOBA_BLOCK == 0
    n_blk = past_len // MOBA_BLOCK
    rows = MOBA_HEADS * n_dec
    qg = q_m.reshape(n_req, n_dec, MOBA_GROUP, LANES).transpose(0, 2, 1, 3)
    half = (np.arange(LANES) // HEAD_DIM)[None, :] == np.arange(MOBA_KV_HEADS)[:, None]
    q_rows = (qg[:, None] * jnp.asarray(half.astype(np.float32))[None, :, None, None, :]).reshape(n_req, rows, LANES)
    k_t = k_cache.transpose(0, 1, 3, 4, 2)
    v_t = v_cache.transpose(0, 1, 3, 4, 2)
    n_ch = past_len // keys
    key_blk = (lax.broadcasted_iota(jnp.int32, (n_ch, n_blk, keys), 0) * (keys // MOBA_BLOCK)
               + lax.broadcasted_iota(jnp.int32, (n_ch, n_blk, keys), 2) // MOBA_BLOCK)
    expand = (key_blk == lax.broadcasted_iota(jnp.int32, (n_ch, n_blk, keys), 1)).astype(BF16)
    page_buf = pltpu.VMEM((MOBA_GATHER_SLOTS, keys // page, MOBA_KV_HEADS, HEAD_DIM, page), F32)
    kern = functools.partial(_moba_decode_kernel, layer, past_len // keys, keys // page, page)
    o_rows = pl.pallas_call(
        kern,
        grid_spec=pltpu.PrefetchScalarGridSpec(
            num_scalar_prefetch=1,
            grid=(n_req,),
            in_specs=[pl.BlockSpec(memory_space=pl.ANY), pl.BlockSpec(memory_space=pl.ANY),
                      pl.BlockSpec((1, rows, LANES), lambda r, pt: (r, 0, 0)),
                      pl.BlockSpec((n_dec, LANES), lambda r, pt: (r, 0)),
                      pl.BlockSpec((n_dec, LANES), lambda r, pt: (r, 0)),
                      pl.BlockSpec(expand.shape, lambda r, pt: (0, 0, 0))],
            out_specs=pl.BlockSpec((1, rows, LANES), lambda r, pt: (r, 0, 0)),
            scratch_shapes=[page_buf, page_buf,
                            pltpu.SemaphoreType.DMA((MOBA_GATHER_SLOTS,)),
                            pltpu.SemaphoreType.DMA((MOBA_GATHER_SLOTS,)),
                            pltpu.VMEM((past_len // keys, LANES, keys), BF16),
                            pltpu.VMEM((past_len // keys, rows, keys), F32)]),
        out_shape=jax.ShapeDtypeStruct((n_req, rows, LANES), F32),
        compiler_params=pltpu.CompilerParams(dimension_semantics=("arbitrary",),
                                             vmem_limit_bytes=VMEM_LIMIT),
        name="moba_decode",
    )(page_table, k_t, v_t, q_rows, k_new, v_new, expand)
    o = o_rows.reshape(n_req, MOBA_KV_HEADS, MOBA_GROUP, n_dec, LANES)
    pairs = jnp.concatenate([o[:, 0, :, :, :HEAD_DIM], o[:, 1, :, :, HEAD_DIM:]], axis=-1)
    return pairs.transpose(0, 2, 1, 3).reshape(n_req * n_dec, MOBA_GROUP * LANES)


def _layer_sample(layer, x2d, lw, caches, state, page_table, n_req, n_dec):
    ck, cv, clat, ckpe = caches
    past_len = page_table.shape[1] * ck.shape[2]
    n = n_req * n_dec
    tile = _tile_rows(n, 256)
    assert tile % n_dec == 0
    pos = jnp.tile(past_len + jnp.arange(n_dec), tile // n_dec)
    q_m, k_m, v_m, q_r, k_r, v_r, gate_r, q_l, c_kv, kpe = _project(x2d, pos, 1, lw, tile)
    o_m = _moba_decode(layer, page_table, ck, cv, q_m, k_m, v_m, n_req, n_dec)
    o_r, st = _retention(q_r, k_r, v_r, gate_r, _state_to_block_diag(state.astype(F32)), lw, n_req, n_dec)
    o_l = _mla_decode(layer, page_table, clat, ckpe, q_l, c_kv, kpe, lw, n_req, n_dec)
    y = _out_mlp(x2d, o_m, o_r, o_l, lw, _tile_rows(n, 512))
    new = (k_m.reshape(n_req, n_dec, MOBA_KV_HEADS, HEAD_DIM), v_m.reshape(n_req, n_dec, MOBA_KV_HEADS, HEAD_DIM),
           c_kv.reshape(n_req, n_dec, MLA_KV_LORA), kpe[:, :MLA_ROPE].reshape(n_req, n_dec, MLA_ROPE),
           _block_diag_to_state(st).astype(state.dtype))
    return y, new


def _layer_prompt(x2d, lw, batch, seq):
    tile = _tile_rows(seq, 256)
    q_m, k_m, v_m, q_r, k_r, v_r, gate_r, q_l, c_kv, kpe = _project(
        x2d, jnp.arange(seq), seq // tile, lw, tile)
    o_m = _moba_prompt(q_m, k_m, v_m, batch, seq)
    st0 = jnp.zeros((batch, RET_HEADS // 2, LANES, LANES), F32)
    o_r, st = _retention(q_r, k_r, v_r, gate_r, st0, lw, batch, seq)
    o_l = _mla_prompt(q_l, c_kv, kpe, lw, batch, seq)
    y = _out_mlp(x2d, o_m, o_r, o_l, lw, _tile_rows(x2d.shape[0], 512))
    new = (k_m.reshape(batch, seq, MOBA_KV_HEADS, HEAD_DIM), v_m.reshape(batch, seq, MOBA_KV_HEADS, HEAD_DIM),
           c_kv.reshape(batch, seq, MLA_KV_LORA), kpe[:, :MLA_ROPE].reshape(batch, seq, MLA_ROPE),
           _block_diag_to_state(st))
    return y, new


def kernel(x_prompt, x_sample, cache_moba_k, cache_moba_v, cache_mla_latent, cache_mla_kpe, state_ret, page_table,
           g_attn, w_in, g_moba_q, g_moba_k, g_ret_out, g_mla_qlat, w_mla_q_up, g_mla_qn, g_mla_qp, g_mla_kvlat,
           w_mla_kv_up, g_mla_kn, g_mla_kp, w_out, g_mlp, w_mlp_up, w_mlp_down):
    batch, seq, d_model = x_prompt.shape
    n_req, n_dec, _ = x_sample.shape
    depth = w_in.shape[0]
    params = (g_attn, w_in, g_moba_q, g_moba_k, g_ret_out, g_mla_qlat, w_mla_q_up, g_mla_qn, g_mla_qp,
              g_mla_kvlat, w_mla_kv_up, g_mla_kn, g_mla_kp, w_out, g_mlp, w_mlp_up, w_mlp_down)
    yp = x_prompt.reshape(batch * seq, d_model)
    ys = x_sample.reshape(n_req * n_dec, d_model)
    caches = (cache_moba_k, cache_moba_v, cache_mla_latent, cache_mla_kpe)
    new_p, new_s = [], []
    for l in range(depth):
        lw = _layer_weights(l, *params)
        yp, st_p = _layer_prompt(yp, lw, batch, seq)
        ys, st_s = _layer_sample(l, ys, lw, caches, state_ret[l], page_table, n_req, n_dec)
        new_p.append(st_p)
        new_s.append(st_s)
    stacked_p = [jnp.stack(a) for a in zip(*new_p)]
    stacked_s = [jnp.stack(a) for a in zip(*new_s)]
    return (yp.reshape(batch, seq, d_model), ys.reshape(n_req, n_dec, d_model), *stacked_p, *stacked_s)
```
